```python
import math
import jax, jax.numpy as jnp
from jax import lax
import numpy as np

D_MODEL = 1024
BATCH = 32
SEQ = 2048
DEPTH = 4

SSM_HEAD_DIM = 64
SSM_HEADS = D_MODEL // SSM_HEAD_DIM
SSM_WIDTH = SSM_HEADS * SSM_HEAD_DIM
SSM_GROUPS = 2
SSM_STATE = 128
SSM_CONV = 7
SSM_CHUNK = 128
ATTN_HEAD_DIM = 64
ATTN_HEADS = D_MODEL // ATTN_HEAD_DIM
ATTN_KV_HEADS = ATTN_HEADS // 4
ATTN_WIDTH = ATTN_HEADS * ATTN_HEAD_DIM
KV_WIDTH = ATTN_KV_HEADS * ATTN_HEAD_DIM
WINDOW = 128
ATTN_BLOCK = 128
KEY_SPAN = ATTN_BLOCK + 2 * WINDOW
REL_BUCKETS = 32
REL_MAX_DIST = 128
MIX_WIDTH = SSM_WIDTH + ATTN_WIDTH
D_FF = 256 * ((8 * D_MODEL // 3 + 255) // 256)
FFN_CONV = 3
NORM_EPS = 1e-6

BC_WIDTH = SSM_GROUPS * SSM_STATE
CONV_CH = SSM_WIDTH + 2 * BC_WIDTH
Z_END = SSM_WIDTH
XBC_END = Z_END + CONV_CH
DT_END = XBC_END + 2 * SSM_HEADS
Q_END = DT_END + ATTN_WIDTH
K_END = Q_END + KV_WIDTH
IN_COLS = K_END + KV_WIDTH

kernel_name = "hymba_ssd_swa_convffn_encoder"


def rms_norm(x, w):
    xf = x.astype(jnp.float32)
    y = xf * lax.rsqrt(jnp.mean(xf * xf, axis=-1, keepdims=True) + NORM_EPS)
    return (y * w.astype(jnp.float32)).astype(x.dtype)


def depthwise_conv_centered(x, w, b):
    k, ch = w.shape
    pad = k // 2
    y = lax.conv_general_dilated(
        x, w[:, None, :].astype(x.dtype), window_strides=(1,), padding=[(pad, pad)],
        dimension_numbers=("NWC", "WIO", "NWC"), feature_group_count=ch)
    return y + b.astype(x.dtype)


def t5_bucket(rel):
    half = REL_BUCKETS // 2
    max_exact = half // 2
    ret = jnp.where(rel > 0, half, 0)
    n = jnp.abs(rel)
    nf = jnp.maximum(n, 1).astype(jnp.float32)
    large = max_exact + (jnp.log(nf / max_exact) / math.log(REL_MAX_DIST / max_exact)
                         * (half - max_exact)).astype(jnp.int32)
    large = jnp.minimum(large, half - 1)
    return ret + jnp.where(n < max_exact, n, large)


def ssd_chunked(x, dt, a, b, c):
    f32 = jnp.float32
    bsz, seq, nh, hp = x.shape
    ng, ns = b.shape[-2:]
    rep = nh // ng
    nc, cl = seq // SSM_CHUNK, SSM_CHUNK
    xdt = (x.astype(f32) * dt[..., None]).reshape(bsz, nc, cl, ng, rep, hp)
    a_dt = (dt * a).reshape(bsz, nc, cl, ng, rep).transpose(0, 3, 4, 1, 2)
    a_cum = jnp.cumsum(a_dt, axis=-1)
    b = b.astype(f32).reshape(bsz, nc, cl, ng, ns)
    c = c.astype(f32).reshape(bsz, nc, cl, ng, ns)
    seg = a_cum[..., :, None] - a_cum[..., None, :]
    lower = jnp.tril(jnp.ones((cl, cl), dtype=bool))
    decay = jnp.exp(jnp.where(lower, seg, -jnp.inf))
    cb = jnp.einsum("bclgn,bcsgn->bgcls", c, b)
    mix = cb[:, :, None] * decay
    y_diag = jnp.einsum("bgrcls,bcsgrp->bclgrp", mix, xdt)
    decay_states = jnp.exp(a_cum[..., -1:] - a_cum)
    states = jnp.einsum("bclgn,bgrcl,bclgrp->cbgrpn", b, decay_states, xdt)
    chunk_decay = jnp.exp(a_cum[..., -1]).transpose(3, 0, 1, 2)

    def step(h, inp):
        st, dec = inp
        return h * dec[..., None, None] + st, h

    h0 = jnp.zeros((bsz, ng, rep, hp, ns), f32)
    _, prev = lax.scan(step, h0, (states, chunk_decay))
    y_off = jnp.einsum("bclgn,cbgrpn,bgrcl->bclgrp", c, prev, jnp.exp(a_cum))
    return (y_diag + y_off).reshape(bsz, seq, nh, hp)


def windowed_gqa(q, k, v, sink, band_bias):
    bsz, seq, nh, hd = q.shape
    nkv = k.shape[2]
    rep = nh // nkv
    nblk = seq // ATTN_BLOCK
    qb_all = q.reshape(bsz, nblk, ATTN_BLOCK, nkv, rep, hd).transpose(1, 0, 2, 3, 4, 5)
    kp = jnp.pad(k, ((0, 0), (WINDOW, WINDOW), (0, 0), (0, 0)))
    vp = jnp.pad(v, ((0, 0), (WINDOW, WINDOW), (0, 0), (0, 0)))
    rel = jnp.arange(KEY_SPAN)[None, :] - WINDOW - jnp.arange(ATTN_BLOCK)[:, None]
    band = jnp.abs(rel) <= WINDOW
    bias = band_bias.reshape(nkv, rep, ATTN_BLOCK, KEY_SPAN)
    sink_l = sink.astype(jnp.float32).reshape(nkv, rep, 1, 1)
    scale = hd ** -0.5

    def one_block(args):
        qb, n = args
        start = n * ATTN_BLOCK
        kb = lax.dynamic_slice_in_dim(kp, start, KEY_SPAN, axis=1)
        vb = lax.dynamic_slice_in_dim(vp, start, KEY_SPAN, axis=1)
        kpos = start - WINDOW + jnp.arange(KEY_SPAN)
        valid = band & ((kpos >= 0) & (kpos < seq))[None, :]
        s = jnp.einsum("bqgrd,bkgd->bgrqk", qb, kb,
                       preferred_element_type=jnp.float32) * scale + bias
        s = jnp.where(valid, s, -jnp.inf)
        m = jnp.maximum(jnp.max(s, axis=-1, keepdims=True), sink_l)
        p = jnp.exp(s - m)
        denom = jnp.sum(p, axis=-1, keepdims=True) + jnp.exp(sink_l - m)
        return jnp.einsum("bgrqk,bkgd->bqgrd", (p / denom).astype(vb.dtype), vb)

    out = lax.map(one_block, (qb_all, jnp.arange(nblk)))
    return out.transpose(1, 0, 2, 3, 4, 5).reshape(bsz, seq, nh * hd)


def setup_inputs(seed: int = 0) -> dict:
    key = jax.random.key(seed)
    ks = jax.random.split(key, 20)
    f32 = jnp.float32

    def nrm(k, shape, scale):
        return jax.random.normal(k, shape, f32) * scale

    dt0 = jnp.exp(jax.random.uniform(ks[6], (DEPTH, 2, SSM_HEADS), f32,
                                     math.log(1e-3), math.log(1e-1)))
    return {
        "x": nrm(ks[0], (BATCH, SEQ, D_MODEL), 1.0),
        "rel_bias": nrm(ks[1], (REL_BUCKETS, ATTN_HEADS), 0.5),
        "norm1_w": 1.0 + nrm(ks[2], (DEPTH, D_MODEL), 0.05),
        "w_in": nrm(ks[3], (DEPTH, D_MODEL, IN_COLS), D_MODEL ** -0.5),
        "conv_w": nrm(ks[4], (DEPTH, SSM_CONV, CONV_CH), SSM_CONV ** -0.5),
        "conv_b": nrm(ks[5], (DEPTH, CONV_CH), 0.01),
        "dt_bias": dt0 + jnp.log(-jnp.expm1(-dt0)),
        "a_log": jnp.log(jax.random.uniform(ks[7], (DEPTH, 2, SSM_HEADS), f32, 1.0, 16.0)),
        "d_skip": 1.0 + nrm(ks[8], (DEPTH, SSM_HEADS), 0.1),
        "ssm_norm_w": 1.0 + nrm(ks[9], (DEPTH, SSM_WIDTH), 0.05),
        "attn_sink": nrm(ks[10], (DEPTH, ATTN_HEADS), 0.5),
        "w_out": nrm(ks[11], (DEPTH, MIX_WIDTH, D_MODEL), MIX_WIDTH ** -0.5),
        "norm2_w": 1.0 + nrm(ks[12], (DEPTH, D_MODEL), 0.05),
        "w_up": nrm(ks[13], (DEPTH, D_MODEL, 2 * D_FF), D_MODEL ** -0.5),
        "ffn_conv_w": nrm(ks[14], (DEPTH, FFN_CONV, D_FF), FFN_CONV ** -0.5),
        "ffn_conv_b": nrm(ks[15], (DEPTH, D_FF), 0.01),
        "w_down": nrm(ks[16], (DEPTH, D_FF, D_MODEL), D_FF ** -0.5),
        "final_norm_w": 1.0 + nrm(ks[17], (D_MODEL,), 0.05),
    }


def reference(x, rel_bias, norm1_w, w_in, conv_w, conv_b, dt_bias, a_log, d_skip,
              ssm_norm_w, attn_sink, w_out, norm2_w, w_up, ffn_conv_w, ffn_conv_b,
              w_down, final_norm_w):
    bsz, seq, _ = x.shape
    f32 = jnp.float32
    rel = jnp.arange(KEY_SPAN)[None, :] - WINDOW - jnp.arange(ATTN_BLOCK)[:, None]
    band_bias = rel_bias.astype(f32)[t5_bucket(rel)].transpose(2, 0, 1)

    for i in range(DEPTH):
        h = rms_norm(x, norm1_w[i])
        proj = h @ w_in[i]
        z, xbc, dt_raw, q, k, v = jnp.split(proj, [Z_END, XBC_END, DT_END, Q_END, K_END], axis=-1)

        xbc = jax.nn.silu(depthwise_conv_centered(xbc, conv_w[i], conv_b[i]))
        xs, bm, cm = jnp.split(xbc, [SSM_WIDTH, SSM_WIDTH + BC_WIDTH], axis=-1)
        xs = xs.reshape(bsz, seq, SSM_HEADS, SSM_HEAD_DIM)
        bm = bm.reshape(bsz, seq, SSM_GROUPS, SSM_STATE)
        cm = cm.reshape(bsz, seq, SSM_GROUPS, SSM_STATE)
        dt = jax.nn.softplus(dt_raw.astype(f32).reshape(bsz, seq, 2, SSM_HEADS)
                             + dt_bias[i].astype(f32))
        a = -jnp.exp(a_log[i].astype(f32))
        y_fwd = ssd_chunked(xs, dt[:, :, 0], a[0], bm, cm)
        y_bwd = jnp.flip(ssd_chunked(jnp.flip(xs, 1), jnp.flip(dt[:, :, 1], 1), a[1],
                                     jnp.flip(bm, 1), jnp.flip(cm, 1)), 1)
        y_ssm = y_fwd + y_bwd + d_skip[i].astype(f32)[:, None] * xs.astype(f32)
        y_ssm = y_ssm.reshape(bsz, seq, SSM_GROUPS, SSM_WIDTH // SSM_GROUPS) \
            * jax.nn.silu(z.astype(f32)).reshape(bsz, seq, SSM_GROUPS, SSM_WIDTH // SSM_GROUPS)
        y_ssm = rms_norm(y_ssm, ssm_norm_w[i].reshape(SSM_GROUPS, -1)).reshape(bsz, seq, SSM_WIDTH)

        y_attn = windowed_gqa(q.reshape(bsz, seq, ATTN_HEADS, ATTN_HEAD_DIM),
                              k.reshape(bsz, seq, ATTN_KV_HEADS, ATTN_HEAD_DIM),
                              v.reshape(bsz, seq, ATTN_KV_HEADS, ATTN_HEAD_DIM),
                              attn_sink[i], band_bias)

        mixed = jnp.concatenate([y_ssm.astype(x.dtype), y_attn.astype(x.dtype)], axis=-1)
        x = x + mixed @ w_out[i]

        h = rms_norm(x, norm2_w[i])
        g, u = jnp.split(h @ w_up[i], [D_FF], axis=-1)
        g = depthwise_conv_centered(g, ffn_conv_w[i], ffn_conv_b[i])
        x = x + (jax.nn.silu(g) * u) @ w_down[i]

    return rms_norm(x, final_norm_w)
```

```python
import functools
import math

import jax
import jax.numpy as jnp
from jax import lax
from jax.experimental import pallas as pl
from jax.experimental.pallas import tpu as pltpu

F32 = jnp.float32
BF16 = jnp.bfloat16

D_MODEL = 1024
HEAD_DIM = 64
SSM_HEADS = 16
SSM_GROUPS = 2
GROUP_HEADS = SSM_HEADS // SSM_GROUPS
GROUP_W = GROUP_HEADS * HEAD_DIM
SSM_STATE = 128
SSM_CONV = 7
CHUNK = 128
ATTN_HEADS = 16
KV_HEADS = 4
KV_REP = ATTN_HEADS // KV_HEADS
WINDOW = 128
QBLK = 128
KEY_SPAN = QBLK + 2 * WINDOW
REL_BUCKETS = 32
REL_MAX_DIST = 128
D_FF = 2816
FFN_CONV = 3
FF_CHUNK = 256
EPS = 1e-6
LANES = 128
CONV_PAD = 8

COL_Z, COL_Q, COL_XS, COL_B, COL_C, COL_K, COL_V = 0, 1024, 2048, 3072, 3328, 3584, 3840
PROJ_COLS = 4096

VMEM_LIMIT = 56 * 1024 * 1024


def _cparams(*sem):
    return pltpu.CompilerParams(dimension_semantics=sem, vmem_limit_bytes=VMEM_LIMIT)


def _sigmoid(x):
    return 1.0 / (1.0 + jnp.exp(-x))


def _softplus(x):
    return jnp.maximum(x, 0.0) + jnp.log1p(jnp.exp(-jnp.abs(x)))


def _rms(x, w):
    ms = jnp.mean(x * x, axis=-1, keepdims=True)
    return x * lax.rsqrt(ms + EPS) * w


def _split3(x):
    hi = x.astype(BF16)
    r1 = x - hi.astype(F32)
    mid = r1.astype(BF16)
    lo = (r1 - mid.astype(F32)).astype(BF16)
    return hi, mid, lo


def _dot(a, b):
    return jnp.dot(a, b, preferred_element_type=F32)


def _dot_nt(a, b):
    return lax.dot_general(a, b, (((1,), (1,)), ((), ())), preferred_element_type=F32)


def _inproj_kernel(x_ref, nw_ref, w_ref, wdt_ref, proj_ref, dt_ref, h_ref):
    @pl.when(pl.program_id(1) == 0)
    def _():
        hb = _rms(x_ref[...], nw_ref[...]).astype(BF16)
        h_ref[...] = hb
        for g in range(SSM_GROUPS):
            dt_ref[g] = _dot(hb, wdt_ref[g])

    proj_ref[...] = _dot(h_ref[...], w_ref[...]).astype(BF16)


def _inproj(x2, nw, w, wdt, tm=1024, tn=1024):
    n = x2.shape[0]
    return pl.pallas_call(
        _inproj_kernel,
        grid=(n // tm, PROJ_COLS // tn),
        in_specs=[
            pl.BlockSpec((tm, D_MODEL), lambda i, j: (i, 0)),
            pl.BlockSpec((1, D_MODEL), lambda i, j: (0, 0)),
            pl.BlockSpec((D_MODEL, tn), lambda i, j: (0, j)),
            pl.BlockSpec((SSM_GROUPS, D_MODEL, LANES), lambda i, j: (0, 0, 0)),
        ],
        out_specs=[
            pl.BlockSpec((tm, tn), lambda i, j: (i, j)),
            pl.BlockSpec((SSM_GROUPS, tm, LANES), lambda i, j: (0, i, 0)),
        ],
        out_shape=[
            jax.ShapeDtypeStruct((n, PROJ_COLS), BF16),
            jax.ShapeDtypeStruct((SSM_GROUPS, n, LANES), F32),
        ],
        scratch_shapes=[pltpu.VMEM((tm, D_MODEL), BF16)],
        compiler_params=_cparams("parallel", "arbitrary"),
        name="inproj",
    )(x2, nw, w, wdt)


def _ssd_kernel(xs_ref, b_ref, c_ref, z_ref, dt_ref, cwx_ref, cwb_ref, cwc_ref, cbx_ref, cbb_ref,
                cbc_ref, dtb_ref, alog_ref, dskip_ref, nw_ref, e_ref, o_ref,
                xpad, bpad, cpad, xs_s, bt_s, c_s, prevf_s, state_s, *, seq):
    nchunk = seq // CHUNK
    lane = lax.broadcasted_iota(jnp.int32, (CHUNK, LANES), 1)
    row = lax.broadcasted_iota(jnp.int32, (CHUNK, LANES), 0)
    is_fwd_lane = lane < GROUP_HEADS
    lt = lane < row
    gt = lane > row
    row2 = lax.broadcasted_iota(jnp.int32, (2 * CHUNK, LANES), 0)
    lane2 = lax.broadcasted_iota(jnp.int32, (2 * CHUNK, LANES), 1)
    tri2 = jnp.where(row2 < CHUNK, jnp.where(lane2 <= row2, 1.0, 0.0),
                     jnp.where(lane2 >= row2 - CHUNK, 1.0, 0.0)).astype(BF16)

    zpad = jnp.zeros((CONV_PAD, GROUP_W), F32)
    xpad[0:CONV_PAD, :] = zpad
    xpad[CONV_PAD + seq:CONV_PAD + seq + CONV_PAD, :] = zpad
    for p in (bpad, cpad):
        p[0:CONV_PAD, :] = zpad[:, :LANES]
        p[CONV_PAD + seq:CONV_PAD + seq + CONV_PAD, :] = zpad[:, :LANES]
    xpad[CONV_PAD:CONV_PAD + seq, :] = xs_ref[...].astype(F32)
    bpad[CONV_PAD:CONV_PAD + seq, :] = b_ref[...].astype(F32)
    cpad[CONV_PAD:CONV_PAD + seq, :] = c_ref[...].astype(F32)

    half = SSM_CONV // 2
    win_rows = CHUNK + 2 * CONV_PAD

    def conv_silu(pad_ref, w_ref, bias_ref, r0):
        win = pad_ref[pl.ds(r0, win_rows), :]
        acc = jnp.broadcast_to(bias_ref[...], (CHUNK, win.shape[1]))
        for k in range(SSM_CONV):
            off = CONV_PAD - half + k
            sh = win if off == 0 else pltpu.roll(win, win_rows - off, axis=0)
            acc = acc + sh[:CHUNK, :] * w_ref[k:k + 1, :]
        return acc * _sigmoid(acc)

    def conv_body(c, carry):
        r0 = pl.multiple_of(c * CHUNK, CHUNK)
        xs_s[pl.ds(r0, CHUNK), :] = conv_silu(xpad, cwx_ref, cbx_ref, r0)
        bt_s[c] = conv_silu(bpad, cwb_ref, cbb_ref, r0).T.astype(BF16)
        c_s[pl.ds(r0, CHUNK), :] = conv_silu(cpad, cwc_ref, cbc_ref, r0).astype(BF16)
        return carry

    lax.fori_loop(0, nchunk, conv_body, 0)

    a_row = -jnp.exp(alog_ref[...])
    e_mat = e_ref[...]

    def expand(q):
        hi = q.astype(BF16)
        lo = (q - hi.astype(F32)).astype(BF16)
        return _dot(hi, e_mat) + _dot(lo, e_mat)

    def chunk_terms(r0):
        dt = _softplus(dt_ref[pl.ds(r0, CHUNK), :] + dtb_ref[...])
        a = dt * a_row
        hi, mid, lo = _split3(a)
        cums = _dot(tri2, hi) + _dot(tri2, mid) + _dot(tri2, lo)
        cum, rcum = cums[:CHUNK], cums[CHUNK:]
        cs = jnp.where(is_fwd_lane, cum, rcum)
        tot = jnp.where(is_fwd_lane[:1], cum[CHUNK - 1:CHUNK], rcum[0:1])
        return dt, cs, tot

    state_s[...] = jnp.zeros_like(state_s)

    def fwd_body(c, carry):
        r0 = pl.multiple_of(c * CHUNK, CHUNK)
        dt, cs, tot = chunk_terms(r0)
        st = state_s[...]
        prevf_s[c] = st.astype(BF16)
        w = jnp.exp(tot - cs) * dt
        xw = (xs_s[pl.ds(r0, CHUNK), :] * expand(w)[:, :GROUP_W]).astype(BF16)
        dec = expand(jnp.broadcast_to(jnp.exp(tot), (8, LANES)))[0:1, :GROUP_W]
        state_s[...] = st * dec + _dot(bt_s[c], xw)
        return carry

    lax.fori_loop(0, nchunk, fwd_body, 0)

    state_s[...] = jnp.zeros_like(state_s)
    dskip = dskip_ref[...]
    nw = nw_ref[...]

    def out_body(i, carry):
        c = nchunk - 1 - i
        r0 = pl.multiple_of(c * CHUNK, CHUNK)
        dt, cs, tot = chunk_terms(r0)
        logdt = jnp.log(dt)
        u_t = (logdt - cs).T
        dsum = dt + pltpu.roll(dt, LANES - GROUP_HEADS, axis=1)
        d_t = jnp.log(dsum).T
        x32 = xs_s[pl.ds(r0, CHUNK), :]
        xb = x32.astype(BF16)
        cc = c_s[pl.ds(r0, CHUNK), :]
        btc = bt_s[c]
        cb = _dot(cc, btc)
        ys = []
        for r in range(GROUP_HEADS):
            arg_f = cs[:, r:r + 1] + u_t[r:r + 1, :]
            arg_b = cs[:, GROUP_HEADS + r:GROUP_HEADS + r + 1] + u_t[GROUP_HEADS + r:GROUP_HEADS + r + 1, :]
            arg = jnp.where(lt, arg_f, jnp.where(gt, arg_b, d_t[r:r + 1, :]))
            mix = (cb * jnp.exp(arg)).astype(BF16)
            ys.append(_dot(mix, xb[:, r * HEAD_DIM:(r + 1) * HEAD_DIM]))
        y = jnp.concatenate(ys, axis=1)
        ee = expand(jnp.exp(cs))
        stb = state_s[...]
        y = y + _dot(cc, prevf_s[c]) * ee[:, :GROUP_W] + _dot(cc, stb.astype(BF16)) * ee[:, GROUP_W:]
        y = y + dskip * x32
        zc = z_ref[pl.ds(r0, CHUNK), :].astype(F32)
        y = y * (zc * _sigmoid(zc))
        o_ref[pl.ds(r0, CHUNK), :] = _rms(y, nw).astype(BF16)
        w = jnp.exp(tot - cs) * dt
        xw = (x32 * expand(w)[:, GROUP_W:]).astype(BF16)
        dec = expand(jnp.broadcast_to(jnp.exp(tot), (8, LANES)))[0:1, GROUP_W:]
        state_s[...] = stb * dec + _dot(btc, xw)
        return carry

    lax.fori_loop(0, nchunk, out_body, 0)


def _ssd(proj, dt, cw, cb, dtb, alog, dskip, nw, e_mat, batch, seq):
    n = batch * seq
    nchunk = seq // CHUNK
    wb = GROUP_W // LANES
    return pl.pallas_call(
        functools.partial(_ssd_kernel, seq=seq),
        grid=(batch, SSM_GROUPS),
        in_specs=[
            pl.BlockSpec((seq, GROUP_W), lambda b, g: (b, COL_XS // GROUP_W + g)),
            pl.BlockSpec((seq, LANES), lambda b, g: (b, COL_B // LANES + g)),
            pl.BlockSpec((seq, LANES), lambda b, g: (b, COL_C // LANES + g)),
            pl.BlockSpec((seq, GROUP_W), lambda b, g: (b, COL_Z // GROUP_W + g)),
            pl.BlockSpec((None, seq, LANES), lambda b, g: (g, b, 0)),
            pl.BlockSpec((SSM_CONV, GROUP_W), lambda b, g: (0, g)),
            pl.BlockSpec((SSM_CONV, LANES), lambda b, g: (0, SSM_GROUPS * wb + g)),
            pl.BlockSpec((SSM_CONV, LANES), lambda b, g: (0, SSM_GROUPS * wb + SSM_GROUPS + g)),
            pl.BlockSpec((1, GROUP_W), lambda b, g: (0, g)),
            pl.BlockSpec((1, LANES), lambda b, g: (0, SSM_GROUPS * wb + g)),
            pl.BlockSpec((1, LANES), lambda b, g: (0, SSM_GROUPS * wb + SSM_GROUPS + g)),
            pl.BlockSpec((None, 1, LANES), lambda b, g: (g, 0, 0)),
            pl.BlockSpec((None, 1, LANES), lambda b, g: (g, 0, 0)),
            pl.BlockSpec((1, GROUP_W), lambda b, g: (0, g)),
            pl.BlockSpec((1, GROUP_W), lambda b, g: (0, g)),
            pl.BlockSpec((LANES, 2 * GROUP_W), lambda b, g: (0, 0)),
        ],
        out_specs=pl.BlockSpec((seq, GROUP_W), lambda b, g: (b, g)),
        out_shape=jax.ShapeDtypeStruct((n, SSM_GROUPS * GROUP_W), BF16),
        scratch_shapes=[
            pltpu.VMEM((seq + 2 * CONV_PAD, GROUP_W), F32),
            pltpu.VMEM((seq + 2 * CONV_PAD, LANES), F32),
            pltpu.VMEM((seq + 2 * CONV_PAD, LANES), F32),
            pltpu.VMEM((seq, GROUP_W), F32),
            pltpu.VMEM((nchunk, SSM_STATE, CHUNK), BF16),
            pltpu.VMEM((seq, SSM_STATE), BF16),
            pltpu.VMEM((nchunk, SSM_STATE, GROUP_W), BF16),
            pltpu.VMEM((SSM_STATE, GROUP_W), F32),
        ],
        compiler_params=_cparams("parallel", "parallel"),
        name="ssd",
    )(proj, proj, proj, proj, dt, cw, cw, cw, cb, cb, cb, dtb, alog, dskip, nw, e_mat)


def _attn_kernel(q_ref, k_ref, v_ref, bias_ref, sink_ref, o_ref, kpad, vpad, *, seq):
    nblk = seq // QBLK
    kvw = KV_HEADS * HEAD_DIM
    zero = jnp.zeros((WINDOW, kvw), BF16)
    for p, src in ((kpad, k_ref), (vpad, v_ref)):
        p[0:WINDOW, :] = zero
        p[WINDOW + seq:WINDOW + seq + WINDOW, :] = zero
        p[WINDOW:WINDOW + seq, :] = src[...]
    kidx = lax.broadcasted_iota(jnp.int32, (1, KEY_SPAN), 1)
    sink = sink_ref[...]

    def body(n, carry):
        r0 = pl.multiple_of(n * QBLK, QBLK)
        kpos = r0 - WINDOW + kidx
        edge = jnp.where((kpos >= 0) & (kpos < seq), 0.0, -jnp.inf)
        kwin = kpad[pl.ds(r0, KEY_SPAN), :]
        vwin = vpad[pl.ds(r0, KEY_SPAN), :]
        outs = []
        for g in range(KV_HEADS):
            kb = kwin[:, g * HEAD_DIM:(g + 1) * HEAD_DIM]
            vb = vwin[:, g * HEAD_DIM:(g + 1) * HEAD_DIM]
            for r in range(KV_REP):
                h = g * KV_REP + r
                qh = q_ref[pl.ds(r0, QBLK), h * HEAD_DIM:(h + 1) * HEAD_DIM]
                s = _dot_nt(qh, kb) + bias_ref[h] + edge
                sk = sink[:, h:h + 1]
                m = jnp.maximum(jnp.max(s, axis=-1, keepdims=True), sk)
                p = jnp.exp(s - m)
                denom = jnp.sum(p, axis=-1, keepdims=True) + jnp.exp(sk - m)
                outs.append(_dot(p.astype(BF16), vb) / denom)
        o_ref[pl.ds(r0, QBLK), :] = jnp.concatenate(outs, axis=1).astype(BF16)
        return carry

    lax.fori_loop(0, nblk, body, 0)


def _attn(proj, bias, sink, batch, seq):
    n = batch * seq
    aw = ATTN_HEADS * HEAD_DIM
    kvw = KV_HEADS * HEAD_DIM
    return pl.pallas_call(
        functools.partial(_attn_kernel, seq=seq),
        grid=(batch,),
        in_specs=[
            pl.BlockSpec((seq, aw), lambda b: (b, COL_Q // aw)),
            pl.BlockSpec((seq, kvw), lambda b: (b, COL_K // kvw)),
            pl.BlockSpec((seq, kvw), lambda b: (b, COL_V // kvw)),
            pl.BlockSpec((ATTN_HEADS, QBLK, KEY_SPAN), lambda b: (0, 0, 0)),
            pl.BlockSpec((1, LANES), lambda b: (0, 0)),
        ],
        out_specs=pl.BlockSpec((seq, aw), lambda b: (b, 0)),
        out_shape=jax.ShapeDtypeStruct((n, aw), BF16),
        scratch_shapes=[
            pltpu.VMEM((seq + 2 * WINDOW, kvw), BF16),
            pltpu.VMEM((seq + 2 * WINDOW, kvw), BF16),
        ],
        compiler_params=_cparams("parallel"),
        name="attn",
    )(proj, proj, proj, bias, sink)


def _outproj_kernel(x_ref, ys_ref, ya_ref, ws_ref, wa_ref, o_ref):
    o_ref[...] = x_ref[...] + _dot(ys_ref[...], ws_ref[...]) + _dot(ya_ref[...], wa_ref[...])


def _outproj(x2, ys, ya, ws, wa, tm=1024):
    n = x2.shape[0]
    return pl.pallas_call(
        _outproj_kernel,
        grid=(n // tm,),
        in_specs=[
            pl.BlockSpec((tm, D_MODEL), lambda i: (i, 0)),
            pl.BlockSpec((tm, D_MODEL), lambda i: (i, 0)),
            pl.BlockSpec((tm, D_MODEL), lambda i: (i, 0)),
            pl.BlockSpec((D_MODEL, D_MODEL), lambda i: (0, 0)),
            pl.BlockSpec((D_MODEL, D_MODEL), lambda i: (0, 0)),
        ],
        out_specs=pl.BlockSpec((tm, D_MODEL), lambda i: (i, 0)),
        out_shape=jax.ShapeDtypeStruct((n, D_MODEL), F32),
        compiler_params=_cparams("parallel"),
        name="outproj",
    )(x2, ys, ya, ws, wa)


def _ffn_kernel(x_ref, nw_ref, wg_ref, wu_ref, cw_ref, cb_ref, wd_ref, o_ref, h_ref, *, seq):
    @pl.when(pl.program_id(1) == 0)
    def _():
        x = x_ref[...]
        h_ref[...] = _rms(x, nw_ref[...]).astype(BF16)
        o_ref[...] = x

    h = h_ref[...]
    g = _dot(h, wg_ref[...])
    u = _dot(h, wu_ref[...])
    row = lax.broadcasted_iota(jnp.int32, g.shape, 0)
    g_prev = jnp.where(row == 0, 0.0, pltpu.roll(g, 1, axis=0))
    g_next = jnp.where(row == seq - 1, 0.0, pltpu.roll(g, seq - 1, axis=0))
    cw = cw_ref[...]
    gc = g_prev * cw[0:1] + g * cw[1:2] + g_next * cw[2:3] + cb_ref[...]
    act = (gc * _sigmoid(gc) * u).astype(BF16)
    o_ref[...] += _dot(act, wd_ref[...])


def _ffn(x2, nw, wup, cw, cb, wd, batch, seq):
    n = batch * seq
    nck = D_FF // FF_CHUNK
    return pl.pallas_call(
        functools.partial(_ffn_kernel, seq=seq),
        grid=(batch, nck),
        in_specs=[
            pl.BlockSpec((seq, D_MODEL), lambda b, j: (b, 0)),
            pl.BlockSpec((1, D_MODEL), lambda b, j: (0, 0)),
            pl.BlockSpec((D_MODEL, FF_CHUNK), lambda b, j: (0, j)),
            pl.BlockSpec((D_MODEL, FF_CHUNK), lambda b, j: (0, nck + j)),
            pl.BlockSpec((FFN_CONV, FF_CHUNK), lambda b, j: (0, j)),
            pl.BlockSpec((1, FF_CHUNK), lambda b, j: (0, j)),
            pl.BlockSpec((FF_CHUNK, D_MODEL), lambda b, j: (j, 0)),
        ],
        out_specs=pl.BlockSpec((seq, D_MODEL), lambda b, j: (b, 0)),
        out_shape=jax.ShapeDtypeStruct((n, D_MODEL), F32),
        scratch_shapes=[pltpu.VMEM((seq, D_MODEL), BF16)],
        compiler_params=_cparams("parallel", "arbitrary"),
        name="ffn",
    )(x2, nw, wup, wup, cw, cb, wd)


def _norm_kernel(x_ref, w_ref, o_ref):
    o_ref[...] = _rms(x_ref[...], w_ref[...])


def _final_norm(x2, w, tm=1024):
    n = x2.shape[0]
    return pl.pallas_call(
        _norm_kernel,
        grid=(n // tm,),
        in_specs=[pl.BlockSpec((tm, D_MODEL), lambda i: (i, 0)),
                  pl.BlockSpec((1, D_MODEL), lambda i: (0, 0))],
        out_specs=pl.BlockSpec((tm, D_MODEL), lambda i: (i, 0)),
        out_shape=jax.ShapeDtypeStruct((n, D_MODEL), F32),
        compiler_params=_cparams("parallel"),
        name="final_norm",
    )(x2, w)


def _t5_bucket(rel):
    half = REL_BUCKETS // 2
    max_exact = half // 2
    ret = jnp.where(rel > 0, half, 0)
    n = jnp.abs(rel)
    nf = jnp.maximum(n, 1).astype(F32)
    large = max_exact + (jnp.log(nf / max_exact) / math.log(REL_MAX_DIST / max_exact)
                         * (half - max_exact)).astype(jnp.int32)
    large = jnp.minimum(large, half - 1)
    return ret + jnp.where(n < max_exact, n, large)


def _band_bias(rel_bias):
    rel = jnp.arange(KEY_SPAN)[None, :] - WINDOW - jnp.arange(QBLK)[:, None]
    bias = rel_bias.astype(F32)[_t5_bucket(rel)].transpose(2, 0, 1)
    return jnp.where((jnp.abs(rel) <= WINDOW)[None], bias, -jnp.inf)


def _head_lanes(p):
    q = p.astype(F32).reshape(2, SSM_GROUPS, GROUP_HEADS).transpose(1, 0, 2).reshape(SSM_GROUPS, 2 * GROUP_HEADS)
    return jnp.pad(q, ((0, 0), (0, LANES - 2 * GROUP_HEADS)))[:, None, :]


def _expand_matrix():
    j = jnp.arange(LANES)[:, None]
    col = jnp.arange(2 * GROUP_W)[None, :]
    return jnp.where((j < 2 * GROUP_HEADS) & (col // HEAD_DIM == j), 1.0, 0.0).astype(BF16)


def kernel(x, rel_bias, norm1_w, w_in, conv_w, conv_b, dt_bias, a_log, d_skip, ssm_norm_w, attn_sink,
           w_out, norm2_w, w_up, ffn_conv_w, ffn_conv_b, w_down, final_norm_w):
    batch, seq, _ = x.shape
    assert seq % CHUNK == 0 and seq % QBLK == 0
    n = batch * seq
    depth = w_in.shape[0]
    x2 = x.reshape(n, D_MODEL)
    bias = _band_bias(rel_bias)
    e_mat = _expand_matrix()
    zw, xw_end = 1024, 2560
    dt_end = xw_end + 2 * SSM_HEADS
    q_end = dt_end + ATTN_HEADS * HEAD_DIM
    k_end = q_end + KV_HEADS * HEAD_DIM
    scale = HEAD_DIM ** -0.5

    for i in range(depth):
        wi = w_in[i]
        w_main = jnp.concatenate(
            [wi[:, :zw], wi[:, dt_end:q_end] * scale, wi[:, zw:xw_end], wi[:, q_end:]], axis=1).astype(BF16)
        wdt = wi[:, xw_end:dt_end].reshape(D_MODEL, 2, SSM_GROUPS, GROUP_HEADS).transpose(2, 0, 1, 3)
        wdt = jnp.pad(wdt.reshape(SSM_GROUPS, D_MODEL, 2 * GROUP_HEADS),
                      ((0, 0), (0, 0), (0, LANES - 2 * GROUP_HEADS))).astype(BF16)
        proj, dt = _inproj(x2, norm1_w[i][None], w_main, wdt)
        y_ssm = _ssd(proj, dt, conv_w[i], conv_b[i][None], _head_lanes(dt_bias[i]), _head_lanes(a_log[i]),
                     jnp.repeat(d_skip[i].astype(F32), HEAD_DIM)[None], ssm_norm_w[i][None], e_mat, batch, seq)
        sink = jnp.pad(attn_sink[i].astype(F32), (0, LANES - ATTN_HEADS))[None]
        y_attn = _attn(proj, bias, sink, batch, seq)
        wo = w_out[i].astype(BF16)
        x2 = _outproj(x2, y_ssm, y_attn, wo[:D_MODEL], wo[D_MODEL:])
        x2 = _ffn(x2, norm2_w[i][None], w_up[i].astype(BF16), ffn_conv_w[i], ffn_conv_b[i][None],
                  w_down[i].astype(BF16), batch, seq)

    return _final_norm(x2, final_norm_w[None]).reshape(batch, seq, D_MODEL)
```

```python
import functools
import math

import jax
import jax.numpy as jnp
from jax import lax
from jax.experimental import pallas as pl
from jax.experimental.pallas import tpu as pltpu

F32 = jnp.float32
BF16 = jnp.bfloat16

D_MODEL = 1024
HEAD_DIM = 64
SSM_HEADS = 16
SSM_GROUPS = 2
GROUP_HEADS = SSM_HEADS // SSM_GROUPS
GROUP_W = GROUP_HEADS * HEAD_DIM
SSM_STATE = 128
SSM_CONV = 7
CHUNK = 128
ATTN_HEADS = 16
KV_HEADS = 4
KV_REP = ATTN_HEADS // KV_HEADS
WINDOW = 128
QBLK = 128
KEY_SPAN = QBLK + 2 * WINDOW
REL_BUCKETS = 32
REL_MAX_DIST = 128
D_FF = 2816
FFN_CONV = 3
FF_CHUNK = 256
EPS = 1e-6
LANES = 128
SUBLANES = 8
CONV_PAD = SUBLANES
HALO_ROWS = 2 * SUBLANES

COL_Z, COL_Q, COL_XS, COL_B, COL_C, COL_K, COL_V = 0, 1024, 2048, 3072, 3328, 3584, 3840
PROJ_COLS = 4096

VMEM_LIMIT = 56 * 1024 * 1024


def _cparams(*sem):
    return pltpu.CompilerParams(dimension_semantics=sem, vmem_limit_bytes=VMEM_LIMIT)


def _sigmoid(x):
    return 1.0 / (1.0 + jnp.exp(-x))


def _softplus(x):
    return jnp.maximum(x, 0.0) + jnp.log1p(jnp.exp(-jnp.abs(x)))


def _rms(x, w):
    ms = jnp.mean(x * x, axis=-1, keepdims=True)
    return x * lax.rsqrt(ms + EPS) * w


def _split3(x):
    hi = x.astype(BF16)
    r1 = x - hi.astype(F32)
    mid = r1.astype(BF16)
    lo = (r1 - mid.astype(F32)).astype(BF16)
    return hi, mid, lo


def _dot(a, b):
    return jnp.dot(a, b, preferred_element_type=F32)


def _dot_nt(a, b):
    return lax.dot_general(a, b, (((1,), (1,)), ((), ())), preferred_element_type=F32)


def _inproj_kernel(x_ref, nw_ref, w_ref, wdt_ref, proj_ref, dt_ref, h_ref):
    @pl.when(pl.program_id(1) == 0)
    def _():
        hb = _rms(x_ref[...], nw_ref[...]).astype(BF16)
        h_ref[...] = hb
        for g in range(SSM_GROUPS):
            dt_ref[g] = _dot(hb, wdt_ref[g])

    proj_ref[...] = _dot(h_ref[...], w_ref[...]).astype(BF16)


def _inproj(x2, nw, w, wdt, tm=1024, tn=1024):
    n = x2.shape[0]
    return pl.pallas_call(
        _inproj_kernel,
        grid=(n // tm, PROJ_COLS // tn),
        in_specs=[
            pl.BlockSpec((tm, D_MODEL), lambda i, j: (i, 0)),
            pl.BlockSpec((1, D_MODEL), lambda i, j: (0, 0)),
            pl.BlockSpec((D_MODEL, tn), lambda i, j: (0, j)),
            pl.BlockSpec((SSM_GROUPS, D_MODEL, LANES), lambda i, j: (0, 0, 0)),
        ],
        out_specs=[
            pl.BlockSpec((tm, tn), lambda i, j: (i, j)),
            pl.BlockSpec((SSM_GROUPS, tm, LANES), lambda i, j: (0, i, 0)),
        ],
        out_shape=[
            jax.ShapeDtypeStruct((n, PROJ_COLS), BF16),
            jax.ShapeDtypeStruct((SSM_GROUPS, n, LANES), F32),
        ],
        scratch_shapes=[pltpu.VMEM((tm, D_MODEL), BF16)],
        compiler_params=_cparams("parallel", "arbitrary"),
        name="inproj",
    )(x2, nw, w, wdt)


def _ssd_kernel(xs_ref, b_ref, c_ref, z_ref, dt_ref, cwx_ref, cwb_ref, cwc_ref, cbx_ref, cbb_ref,
                cbc_ref, dtb_ref, alog_ref, dskip_ref, nw_ref, e_ref, o_ref,
                xpad, bpad, cpad, xs_s, bt_s, c_s, cs_s, dts_s, tot_s, dec_s, sf_s, sb_s, state_s, *, seq):
    nchunk = seq // CHUNK
    npair = GROUP_W // LANES
    lane = lax.broadcasted_iota(jnp.int32, (CHUNK, LANES), 1)
    row = lax.broadcasted_iota(jnp.int32, (CHUNK, LANES), 0)
    is_fwd_lane = lane < GROUP_HEADS
    lo_half = lane < HEAD_DIM
    lt = lane < row
    gt = lane > row
    row2 = lax.broadcasted_iota(jnp.int32, (2 * CHUNK, LANES), 0)
    lane2 = lax.broadcasted_iota(jnp.int32, (2 * CHUNK, LANES), 1)
    tri2 = jnp.where(row2 < CHUNK, jnp.where(lane2 <= row2, 1.0, 0.0),
                     jnp.where(lane2 >= row2 - CHUNK, 1.0, 0.0)).astype(BF16)

    def window_body(c, carry):
        r0 = pl.multiple_of(c * CHUNK, CHUNK)
        lo = pl.multiple_of(jnp.maximum(r0 - HALO_ROWS, 0), HALO_ROWS)
        hi = pl.multiple_of(jnp.minimum(r0 + CHUNK, seq - HALO_ROWS), HALO_ROWS)
        for src, dst in ((xs_ref, xpad), (b_ref, bpad), (c_ref, cpad)):
            prev = src[pl.ds(lo, HALO_ROWS), :].astype(F32)[HALO_ROWS - CONV_PAD:]
            nxt = src[pl.ds(hi, HALO_ROWS), :].astype(F32)[:CONV_PAD]
            dst[c, 0:CONV_PAD, :] = jnp.where(c > 0, prev, 0.0)
            dst[c, CONV_PAD:CONV_PAD + CHUNK, :] = src[pl.ds(r0, CHUNK), :].astype(F32)
            dst[c, CONV_PAD + CHUNK:CONV_PAD + CHUNK + CONV_PAD, :] = jnp.where(c < nchunk - 1, nxt, 0.0)
        return carry

    lax.fori_loop(0, nchunk, window_body, 0)

    half = SSM_CONV // 2

    def conv_silu(pad_ref, w_ref, bias_ref, c, j):
        cols = slice(j * LANES, (j + 1) * LANES)
        acc = jnp.broadcast_to(bias_ref[:, cols], (CHUNK, LANES))
        for k in range(SSM_CONV):
            off = CONV_PAD - half + k
            acc = acc + pad_ref[c, off:off + CHUNK, cols] * w_ref[k:k + 1, cols]
        return acc * _sigmoid(acc)

    a_row = -jnp.exp(alog_ref[...])
    e_mat = e_ref[...]

    def expand(q):
        hi = q.astype(BF16)
        lo = (q - hi.astype(F32)).astype(BF16)
        return _dot(hi, e_mat) + _dot(lo, e_mat)

    def prep_body(c, carry):
        r0 = pl.multiple_of(c * CHUNK, CHUNK)
        x32 = jnp.concatenate([conv_silu(xpad, cwx_ref, cbx_ref, c, j) for j in range(npair)], axis=1)
        xs_s[pl.ds(r0, CHUNK), :] = x32
        btc = conv_silu(bpad, cwb_ref, cbb_ref, c, 0).T.astype(BF16)
        bt_s[c] = btc
        c_s[pl.ds(r0, CHUNK), :] = conv_silu(cpad, cwc_ref, cbc_ref, c, 0).astype(BF16)

        dt = _softplus(dt_ref[pl.ds(r0, CHUNK), :] + dtb_ref[...])
        hi, mid, lo = _split3(dt * a_row)
        cums = _dot(tri2, hi) + _dot(tri2, mid) + _dot(tri2, lo)
        cum, rcum = cums[:CHUNK], cums[CHUNK:]
        cs = jnp.where(is_fwd_lane, cum, rcum)
        tot = jnp.where(is_fwd_lane[:1], cum[CHUNK - 1:CHUNK], rcum[0:1])
        cs_s[pl.ds(r0, CHUNK), :] = cs
        dts_s[pl.ds(r0, CHUNK), :] = dt
        tot8 = jnp.broadcast_to(tot, (SUBLANES, LANES))
        tot_s[c] = tot8
        dec_s[c] = expand(jnp.exp(tot8))
        ew = expand(jnp.exp(tot - cs) * dt)
        sf_s[c] = _dot(btc, (x32 * ew[:, :GROUP_W]).astype(BF16))
        sb_s[c] = _dot(btc, (x32 * ew[:, GROUP_W:]).astype(BF16))
        return carry

    lax.fori_loop(0, nchunk, prep_body, 0, unroll=2)

    def scan(s_ref, lanes, reverse):
        state_s[...] = jnp.zeros_like(state_s)

        def body(i, carry):
            c = nchunk - 1 - i if reverse else i
            st = state_s[...]
            contrib = s_ref[c]
            s_ref[c] = st
            state_s[...] = st * dec_s[c][0:1, lanes] + contrib
            return carry

        lax.fori_loop(0, nchunk, body, 0)

    scan(sf_s, slice(0, GROUP_W), False)
    scan(sb_s, slice(GROUP_W, 2 * GROUP_W), True)

    dskip = dskip_ref[...]
    nw = nw_ref[...]

    def out_body(c, carry):
        r0 = pl.multiple_of(c * CHUNK, CHUNK)
        dt = dts_s[pl.ds(r0, CHUNK), :]
        cs = cs_s[pl.ds(r0, CHUNK), :]
        u_t = (jnp.log(dt) - cs).T
        dsum = dt + pltpu.roll(dt, LANES - GROUP_HEADS, axis=1)
        d_t = jnp.log(dsum).T
        x32 = xs_s[pl.ds(r0, CHUNK), :]
        cc = c_s[pl.ds(r0, CHUNK), :]
        cb = _dot(cc, bt_s[c])
        ys = []
        for j in range(npair):
            xp = x32[:, j * LANES:(j + 1) * LANES]
            x_halves = (jnp.where(lo_half, xp, 0.0).astype(BF16), jnp.where(lo_half, 0.0, xp).astype(BF16))
            acc = None
            for par in range(2):
                r = 2 * j + par
                arg_f = cs[:, r:r + 1] + u_t[r:r + 1, :]
                arg_b = cs[:, GROUP_HEADS + r:GROUP_HEADS + r + 1] + u_t[GROUP_HEADS + r:GROUP_HEADS + r + 1, :]
                arg = jnp.where(lt, arg_f, jnp.where(gt, arg_b, d_t[r:r + 1, :]))
                mix = (cb * jnp.exp(arg)).astype(BF16)
                term = _dot(mix, x_halves[par])
                acc = term if acc is None else acc + term
            ys.append(acc)
        y = jnp.concatenate(ys, axis=1)
        ee = expand(jnp.exp(cs))
        y = y + _dot(cc, sf_s[c].astype(BF16)) * ee[:, :GROUP_W] + _dot(cc, sb_s[c].astype(BF16)) * ee[:, GROUP_W:]
        y = y + dskip * x32
        zc = z_ref[pl.ds(r0, CHUNK), :].astype(F32)
        y = y * (zc * _sigmoid(zc))
        o_ref[pl.ds(r0, CHUNK), :] = _rms(y, nw).astype(BF16)
        return carry

    lax.fori_loop(0, nchunk, out_body, 0, unroll=2)


def _ssd(proj, dt, cw, cb, dtb, alog, dskip, nw, e_mat, batch, seq):
    n = batch * seq
    nchunk = seq // CHUNK
    wb = GROUP_W // LANES
    return pl.pallas_call(
        functools.partial(_ssd_kernel, seq=seq),
        grid=(batch, SSM_GROUPS),
        in_specs=[
            pl.BlockSpec((seq, GROUP_W), lambda b, g: (b, COL_XS // GROUP_W + g)),
            pl.BlockSpec((seq, LANES), lambda b, g: (b, COL_B // LANES + g)),
            pl.BlockSpec((seq, LANES), lambda b, g: (b, COL_C // LANES + g)),
            pl.BlockSpec((seq, GROUP_W), lambda b, g: (b, COL_Z // GROUP_W + g)),
            pl.BlockSpec((None, seq, LANES), lambda b, g: (g, b, 0)),
            pl.BlockSpec((SSM_CONV, GROUP_W), lambda b, g: (0, g)),
            pl.BlockSpec((SSM_CONV, LANES), lambda b, g: (0, SSM_GROUPS * wb + g)),
            pl.BlockSpec((SSM_CONV, LANES), lambda b, g: (0, SSM_GROUPS * wb + SSM_GROUPS + g)),
            pl.BlockSpec((1, GROUP_W), lambda b, g: (0, g)),
            pl.BlockSpec((1, LANES), lambda b, g: (0, SSM_GROUPS * wb + g)),
            pl.BlockSpec((1, LANES), lambda b, g: (0, SSM_GROUPS * wb + SSM_GROUPS + g)),
            pl.BlockSpec((None, 1, LANES), lambda b, g: (g, 0, 0)),
            pl.BlockSpec((None, 1, LANES), lambda b, g: (g, 0, 0)),
            pl.BlockSpec((1, GROUP_W), lambda b, g: (0, g)),
            pl.BlockSpec((1, GROUP_W), lambda b, g: (0, g)),
            pl.BlockSpec((LANES, 2 * GROUP_W), lambda b, g: (0, 0)),
        ],
        out_specs=pl.BlockSpec((seq, GROUP_W), lambda b, g: (b, g)),
        out_shape=jax.ShapeDtypeStruct((n, SSM_GROUPS * GROUP_W), BF16),
        scratch_shapes=[
            pltpu.VMEM((nchunk, CHUNK + 2 * CONV_PAD, GROUP_W), F32),
            pltpu.VMEM((nchunk, CHUNK + 2 * CONV_PAD, LANES), F32),
            pltpu.VMEM((nchunk, CHUNK + 2 * CONV_PAD, LANES), F32),
            pltpu.VMEM((seq, GROUP_W), F32),
            pltpu.VMEM((nchunk, SSM_STATE, CHUNK), BF16),
            pltpu.VMEM((seq, SSM_STATE), BF16),
            pltpu.VMEM((seq, LANES), F32),
            pltpu.VMEM((seq, LANES), F32),
            pltpu.VMEM((nchunk, SUBLANES, LANES), F32),
            pltpu.VMEM((nchunk, SUBLANES, 2 * GROUP_W), F32),
            pltpu.VMEM((nchunk, SSM_STATE, GROUP_W), F32),
            pltpu.VMEM((nchunk, SSM_STATE, GROUP_W), F32),
            pltpu.VMEM((SSM_STATE, GROUP_W), F32),
        ],
        compiler_params=_cparams("parallel", "parallel"),
        name="ssd",
    )(proj, proj, proj, proj, dt, cw, cw, cw, cb, cb, cb, dtb, alog, dskip, nw, e_mat)


def _attn_kernel(sink_ref, q_ref, k_ref, v_ref, bias_ref, o_ref, ke_s, ko_s, vt_s, *, seq):
    nblk = seq // QBLK
    kvw = KV_HEADS * HEAD_DIM
    ncol = kvw // LANES
    lane = lax.broadcasted_iota(jnp.int32, (QBLK, LANES), 1)
    lo_half = lane < HEAD_DIM

    zrow = jnp.zeros((WINDOW, KV_HEADS * LANES), BF16)
    for s in (ke_s, ko_s):
        s[0:WINDOW, :] = zrow
        s[WINDOW + seq:WINDOW + seq + WINDOW, :] = zrow
    zcol = jnp.zeros((kvw, WINDOW), BF16)
    vt_s[0] = zcol
    vt_s[nblk + 1] = zcol

    def fill_body(c, carry):
        r0 = pl.multiple_of(c * QBLK, QBLK)
        dst = pl.ds(WINDOW + r0, QBLK)
        kc = k_ref[pl.ds(r0, QBLK), :].astype(F32)
        for j in range(ncol):
            col = kc[:, j * LANES:(j + 1) * LANES]
            swp = pltpu.roll(col, HEAD_DIM, axis=1)
            g0, g1 = 2 * j, 2 * j + 1
            ke_s[dst, g0 * LANES:(g0 + 1) * LANES] = jnp.where(lo_half, col, 0.0).astype(BF16)
            ko_s[dst, g0 * LANES:(g0 + 1) * LANES] = jnp.where(lo_half, 0.0, swp).astype(BF16)
            ke_s[dst, g1 * LANES:(g1 + 1) * LANES] = jnp.where(lo_half, swp, 0.0).astype(BF16)
            ko_s[dst, g1 * LANES:(g1 + 1) * LANES] = jnp.where(lo_half, 0.0, col).astype(BF16)
        vt_s[c + 1] = v_ref[pl.ds(r0, QBLK), :].astype(F32).T.astype(BF16)
        return carry

    lax.fori_loop(0, nblk, fill_body, 0)

    krow = lax.broadcasted_iota(jnp.int32, (KEY_SPAN, LANES), 0)

    def body(n, carry):
        r0 = pl.multiple_of(n * QBLK, QBLK)
        kpos = r0 - WINDOW + krow
        edge = jnp.where((kpos >= 0) & (kpos < seq), 0.0, -jnp.inf)
        outs = []
        for pr in range(ATTN_HEADS // 2):
            qp = q_ref[pl.ds(r0, QBLK), pr * LANES:(pr + 1) * LANES]
            g = (2 * pr) // KV_REP
            vts = [vt_s[n + w, g * HEAD_DIM:(g + 1) * HEAD_DIM, :] for w in range(KEY_SPAN // QBLK)]
            for par, ks in ((0, ke_s), (1, ko_s)):
                h = 2 * pr + par
                kk = ks[pl.ds(r0, KEY_SPAN), g * LANES:(g + 1) * LANES]
                t = _dot_nt(kk, qp) + bias_ref[h] + edge
                sk = sink_ref[h]
                m = jnp.maximum(jnp.max(t, axis=0, keepdims=True), sk)
                p = jnp.exp(t - m)
                denom = jnp.sum(p, axis=0, keepdims=True) + jnp.exp(sk - m)
                pb = p.astype(BF16)
                pv = sum(_dot(vts[w], pb[w * QBLK:(w + 1) * QBLK]) for w in range(KEY_SPAN // QBLK))
                outs.append(pv * (1.0 / denom))
        o_ref[pl.ds(r0, QBLK), :] = jnp.concatenate(outs, axis=0).T.astype(BF16)
        return carry

    lax.fori_loop(0, nblk, body, 0)


def _attn(proj, bias_t, sink, batch, seq):
    n = batch * seq
    aw = ATTN_HEADS * HEAD_DIM
    kvw = KV_HEADS * HEAD_DIM
    return pl.pallas_call(
        functools.partial(_attn_kernel, seq=seq),
        grid=(batch,),
        in_specs=[
            pl.BlockSpec(memory_space=pltpu.SMEM),
            pl.BlockSpec((seq, aw), lambda b: (b, COL_Q // aw)),
            pl.BlockSpec((seq, kvw), lambda b: (b, COL_K // kvw)),
            pl.BlockSpec((seq, kvw), lambda b: (b, COL_V // kvw)),
            pl.BlockSpec((ATTN_HEADS, KEY_SPAN, QBLK), lambda b: (0, 0, 0)),
        ],
        out_specs=pl.BlockSpec((seq, aw), lambda b: (b, 0)),
        out_shape=jax.ShapeDtypeStruct((n, aw), BF16),
        scratch_shapes=[
            pltpu.VMEM((seq + 2 * WINDOW, KV_HEADS * LANES), BF16),
            pltpu.VMEM((seq + 2 * WINDOW, KV_HEADS * LANES), BF16),
            pltpu.VMEM((seq // QBLK + 2, kvw, QBLK), BF16),
        ],
        compiler_params=_cparams("parallel"),
        name="attn",
    )(sink, proj, proj, proj, bias_t)


def _outproj_kernel(x_ref, ys_ref, ya_ref, ws_ref, wa_ref, o_ref):
    o_ref[...] = x_ref[...] + _dot(ys_ref[...], ws_ref[...]) + _dot(ya_ref[...], wa_ref[...])


def _outproj(x2, ys, ya, ws, wa, tm=1024):
    n = x2.shape[0]
    return pl.pallas_call(
        _outproj_kernel,
        grid=(n // tm,),
        in_specs=[
            pl.BlockSpec((tm, D_MODEL), lambda i: (i, 0)),
            pl.BlockSpec((tm, D_MODEL), lambda i: (i, 0)),
            pl.BlockSpec((tm, D_MODEL), lambda i: (i, 0)),
            pl.BlockSpec((D_MODEL, D_MODEL), lambda i: (0, 0)),
            pl.BlockSpec((D_MODEL, D_MODEL), lambda i: (0, 0)),
        ],
        out_specs=pl.BlockSpec((tm, D_MODEL), lambda i: (i, 0)),
        out_shape=jax.ShapeDtypeStruct((n, D_MODEL), F32),
        compiler_params=_cparams("parallel"),
        name="outproj",
    )(x2, ys, ya, ws, wa)


def _ffn_kernel(x_ref, nw_ref, wg_ref, wu_ref, cw_ref, cb_ref, wd_ref, o_ref, h_ref, *, seq):
    @pl.when(pl.program_id(1) == 0)
    def _():
        x = x_ref[...]
        h_ref[...] = _rms(x, nw_ref[...]).astype(BF16)
        o_ref[...] = x

    h = h_ref[...]
    g = _dot(h, wg_ref[...])
    u = _dot(h, wu_ref[...])
    row = lax.broadcasted_iota(jnp.int32, g.shape, 0)
    g_prev = jnp.where(row == 0, 0.0, pltpu.roll(g, 1, axis=0))
    g_next = jnp.where(row == seq - 1, 0.0, pltpu.roll(g, seq - 1, axis=0))
    cw = cw_ref[...]
    gc = g_prev * cw[0:1] + g * cw[1:2] + g_next * cw[2:3] + cb_ref[...]
    act = (gc * _sigmoid(gc) * u).astype(BF16)
    o_ref[...] += _dot(act, wd_ref[...])


def _ffn(x2, nw, wup, cw, cb, wd, batch, seq):
    n = batch * seq
    nck = D_FF // FF_CHUNK
    return pl.pallas_call(
        functools.partial(_ffn_kernel, seq=seq),
        grid=(batch, nck),
        in_specs=[
            pl.BlockSpec((seq, D_MODEL), lambda b, j: (b, 0)),
            pl.BlockSpec((1, D_MODEL), lambda b, j: (0, 0)),
            pl.BlockSpec((D_MODEL, FF_CHUNK), lambda b, j: (0, j)),
            pl.BlockSpec((D_MODEL, FF_CHUNK), lambda b, j: (0, nck + j)),
            pl.BlockSpec((FFN_CONV, FF_CHUNK), lambda b, j: (0, j)),
            pl.BlockSpec((1, FF_CHUNK), lambda b, j: (0, j)),
            pl.BlockSpec((FF_CHUNK, D_MODEL), lambda b, j: (j, 0)),
        ],
        out_specs=pl.BlockSpec((seq, D_MODEL), lambda b, j: (b, 0)),
        out_shape=jax.ShapeDtypeStruct((n, D_MODEL), F32),
        scratch_shapes=[pltpu.VMEM((seq, D_MODEL), BF16)],
        compiler_params=_cparams("parallel", "arbitrary"),
        name="ffn",
    )(x2, nw, wup, wup, cw, cb, wd)


def _norm_kernel(x_ref, w_ref, o_ref):
    o_ref[...] = _rms(x_ref[...], w_ref[...])


def _final_norm(x2, w, tm=1024):
    n = x2.shape[0]
    return pl.pallas_call(
        _norm_kernel,
        grid=(n // tm,),
        in_specs=[pl.BlockSpec((tm, D_MODEL), lambda i: (i, 0)),
                  pl.BlockSpec((1, D_MODEL), lambda i: (0, 0))],
        out_specs=pl.BlockSpec((tm, D_MODEL), lambda i: (i, 0)),
        out_shape=jax.ShapeDtypeStruct((n, D_MODEL), F32),
        compiler_params=_cparams("parallel"),
        name="final_norm",
    )(x2, w)


def _t5_bucket(rel):
    half = REL_BUCKETS // 2
    max_exact = half // 2
    ret = jnp.where(rel > 0, half, 0)
    n = jnp.abs(rel)
    nf = jnp.maximum(n, 1).astype(F32)
    large = max_exact + (jnp.log(nf / max_exact) / math.log(REL_MAX_DIST / max_exact)
                         * (half - max_exact)).astype(jnp.int32)
    large = jnp.minimum(large, half - 1)
    return ret + jnp.where(n < max_exact, n, large)


def _band_bias_t(rel_bias):
    rel = jnp.arange(KEY_SPAN)[:, None] - WINDOW - jnp.arange(QBLK)[None, :]
    bias = rel_bias.astype(F32)[_t5_bucket(rel)].transpose(2, 0, 1)
    return jnp.where((jnp.abs(rel) <= WINDOW)[None], bias, -jnp.inf)


def _head_lanes(p):
    q = p.astype(F32).reshape(2, SSM_GROUPS, GROUP_HEADS).transpose(1, 0, 2).reshape(SSM_GROUPS, 2 * GROUP_HEADS)
    return jnp.pad(q, ((0, 0), (0, LANES - 2 * GROUP_HEADS)))[:, None, :]


def _expand_matrix():
    j = jnp.arange(LANES)[:, None]
    col = jnp.arange(2 * GROUP_W)[None, :]
    return jnp.where((j < 2 * GROUP_HEADS) & (col // HEAD_DIM == j), 1.0, 0.0).astype(BF16)


def kernel(x, rel_bias, norm1_w, w_in, conv_w, conv_b, dt_bias, a_log, d_skip, ssm_norm_w, attn_sink,
           w_out, norm2_w, w_up, ffn_conv_w, ffn_conv_b, w_down, final_norm_w):
    batch, seq, _ = x.shape
    assert seq % CHUNK == 0 and seq % QBLK == 0
    n = batch * seq
    depth = w_in.shape[0]
    x2 = x.reshape(n, D_MODEL)
    bias_t = _band_bias_t(rel_bias)
    e_mat = _expand_matrix()
    zw, xw_end = 1024, 2560
    dt_end = xw_end + 2 * SSM_HEADS
    q_end = dt_end + ATTN_HEADS * HEAD_DIM
    scale = HEAD_DIM ** -0.5

    for i in range(depth):
        wi = w_in[i]
        w_main = jnp.concatenate(
            [wi[:, :zw], wi[:, dt_end:q_end] * scale, wi[:, zw:xw_end], wi[:, q_end:]], axis=1).astype(BF16)
        wdt = wi[:, xw_end:dt_end].reshape(D_MODEL, 2, SSM_GROUPS, GROUP_HEADS).transpose(2, 0, 1, 3)
        wdt = jnp.pad(wdt.reshape(SSM_GROUPS, D_MODEL, 2 * GROUP_HEADS),
                      ((0, 0), (0, 0), (0, LANES - 2 * GROUP_HEADS))).astype(BF16)
        proj, dt = _inproj(x2, norm1_w[i][None], w_main, wdt)
        y_ssm = _ssd(proj, dt, conv_w[i], conv_b[i][None], _head_lanes(dt_bias[i]), _head_lanes(a_log[i]),
                     jnp.repeat(d_skip[i].astype(F32), HEAD_DIM)[None], ssm_norm_w[i][None], e_mat, batch, seq)
        y_attn = _attn(proj, bias_t, attn_sink[i].astype(F32), batch, seq)
        wo = w_out[i].astype(BF16)
        x2 = _outproj(x2, y_ssm, y_attn, wo[:D_MODEL], wo[D_MODEL:])
        x2 = _ffn(x2, norm2_w[i][None], w_up[i].astype(BF16), ffn_conv_w[i], ffn_conv_b[i][None],
                  w_down[i].astype(BF16), batch, seq)

    return _final_norm(x2, final_norm_w[None]).reshape(batch, seq, D_MODEL)
```

```python
import functools
import math

import jax
import jax.numpy as jnp
from jax import lax
from jax.experimental import pallas as pl
from jax.experimental.pallas import tpu as pltpu

F32 = jnp.float32
BF16 = jnp.bfloat16

D_MODEL = 1024
HEAD_DIM = 64
SSM_HEADS = 16
SSM_GROUPS = 2
GROUP_HEADS = SSM_HEADS // SSM_GROUPS
GROUP_W = GROUP_HEADS * HEAD_DIM
SSM_STATE = 128
SSM_CONV = 7
CHUNK = 128
ATTN_HEADS = 16
KV_HEADS = 4
KV_REP = ATTN_HEADS // KV_HEADS
WINDOW = 128
QBLK = 128
KEY_SPAN = QBLK + 2 * WINDOW
KEY_TILES = KEY_SPAN // QBLK
REL_BUCKETS = 32
REL_MAX_DIST = 128
D_FF = 2816
FFN_CONV = 3
FF_CHUNK = 256
EPS = 1e-6
LANES = 128
SUBLANES = 8
HALO_ROWS = 2 * SUBLANES

COL_Z, COL_Q, COL_XS, COL_B, COL_C, COL_K, COL_V = 0, 1024, 2048, 3072, 3328, 3584, 3840
PROJ_COLS = 4096

VMEM_LIMIT = 56 * 1024 * 1024


def _cparams(*sem):
    return pltpu.CompilerParams(dimension_semantics=sem, vmem_limit_bytes=VMEM_LIMIT)


def _sigmoid(x):
    return 1.0 / (1.0 + jnp.exp(-x))


def _softplus(x):
    return jnp.maximum(x, 0.0) + jnp.log1p(jnp.exp(-jnp.abs(x)))


def _rms(x, w):
    ms = jnp.mean(x * x, axis=-1, keepdims=True)
    return x * lax.rsqrt(ms + EPS) * w


def _split3(x):
    hi = x.astype(BF16)
    r1 = x - hi.astype(F32)
    mid = r1.astype(BF16)
    lo = (r1 - mid.astype(F32)).astype(BF16)
    return hi, mid, lo


def _dot(a, b):
    return jnp.dot(a, b, preferred_element_type=F32)


def _dot_nt(a, b):
    return lax.dot_general(a, b, (((1,), (1,)), ((), ())), preferred_element_type=F32)


def _inproj_kernel(x_ref, nw_ref, w_ref, wdt_ref, proj_ref, dt_ref, h_ref):
    @pl.when(pl.program_id(1) == 0)
    def _():
        hb = _rms(x_ref[...], nw_ref[...]).astype(BF16)
        h_ref[...] = hb
        for g in range(SSM_GROUPS):
            dt_ref[g] = _dot(hb, wdt_ref[g])

    proj_ref[...] = _dot(h_ref[...], w_ref[...]).astype(BF16)


def _inproj(x2, nw, w, wdt, tm=1024, tn=1024):
    n = x2.shape[0]
    return pl.pallas_call(
        _inproj_kernel,
        grid=(n // tm, PROJ_COLS // tn),
        in_specs=[
            pl.BlockSpec((tm, D_MODEL), lambda i, j: (i, 0)),
            pl.BlockSpec((1, D_MODEL), lambda i, j: (0, 0)),
            pl.BlockSpec((D_MODEL, tn), lambda i, j: (0, j)),
            pl.BlockSpec((SSM_GROUPS, D_MODEL, LANES), lambda i, j: (0, 0, 0)),
        ],
        out_specs=[
            pl.BlockSpec((tm, tn), lambda i, j: (i, j)),
            pl.BlockSpec((SSM_GROUPS, tm, LANES), lambda i, j: (0, i, 0)),
        ],
        out_shape=[
            jax.ShapeDtypeStruct((n, PROJ_COLS), BF16),
            jax.ShapeDtypeStruct((SSM_GROUPS, n, LANES), F32),
        ],
        scratch_shapes=[pltpu.VMEM((tm, D_MODEL), BF16)],
        compiler_params=_cparams("parallel", "arbitrary"),
        name="inproj",
    )(x2, nw, w, wdt)


def _ssd_kernel(xs_ref, b_ref, c_ref, dt_ref, cwx_ref, cwb_ref, cwc_ref, cbx_ref, cbb_ref,
                cbc_ref, dtb_ref, alog_ref, dskip_ref, e_ref, o_ref,
                xs_s, bt_s, c_s, cs_s, dts_s, dec_s, sf_s, sb_s, state_s, *, seq):
    nchunk = seq // CHUNK
    npair = GROUP_W // LANES
    lane = lax.broadcasted_iota(jnp.int32, (CHUNK, LANES), 1)
    row = lax.broadcasted_iota(jnp.int32, (CHUNK, LANES), 0)
    is_fwd_lane = lane < GROUP_HEADS
    lo_half = lane < HEAD_DIM
    lt = lane < row
    gt = lane > row
    row2 = lax.broadcasted_iota(jnp.int32, (2 * CHUNK, LANES), 0)
    lane2 = lax.broadcasted_iota(jnp.int32, (2 * CHUNK, LANES), 1)
    tri2 = jnp.where(row2 < CHUNK, jnp.where(lane2 <= row2, 1.0, 0.0),
                     jnp.where(lane2 >= row2 - CHUNK, 1.0, 0.0)).astype(BF16)

    half = SSM_CONV // 2
    win_rows = CHUNK + 2 * HALO_ROWS
    srow = lax.broadcasted_iota(jnp.int32, (half * CHUNK, win_rows), 0)
    scol = lax.broadcasted_iota(jnp.int32, (half * CHUNK, win_rows), 1)
    sblk = jnp.right_shift(srow, CHUNK.bit_length() - 1)
    shift_mat = jnp.where(scol == srow - sblk * CHUNK + (HALO_ROWS - half) + sblk, 1.0, 0.0).astype(BF16)

    def conv_silu(src, w_ref, bias_ref, c, r0):
        lo = pl.multiple_of(jnp.maximum(r0 - HALO_ROWS, 0), HALO_ROWS)
        hi = pl.multiple_of(jnp.minimum(r0 + CHUNK, seq - HALO_ROWS), HALO_ROWS)
        prev = jnp.where(c > 0, src[pl.ds(lo, HALO_ROWS), :].astype(F32), 0.0)
        nxt = jnp.where(c < nchunk - 1, src[pl.ds(hi, HALO_ROWS), :].astype(F32), 0.0)
        cur = src[pl.ds(r0, CHUNK), :]
        curf = cur.astype(F32)
        win = jnp.concatenate([prev.astype(BF16), cur, nxt.astype(BF16)], axis=0)
        before = _dot(shift_mat, win)
        tail = jnp.concatenate([curf, nxt], axis=0)
        acc = bias_ref[...] + curf * w_ref[half:half + 1, :]
        for k in range(half):
            acc = acc + before[k * CHUNK:(k + 1) * CHUNK] * w_ref[k:k + 1, :]
        for k in range(half + 1, SSM_CONV):
            acc = acc + tail[k - half:k - half + CHUNK] * w_ref[k:k + 1, :]
        return acc * _sigmoid(acc)

    a_row = -jnp.exp(alog_ref[...])
    e_mat = e_ref[...]

    def expand(q):
        hi = q.astype(BF16)
        lo = (q - hi.astype(F32)).astype(BF16)
        return _dot(hi, e_mat) + _dot(lo, e_mat)

    def prep_body(c, carry):
        r0 = pl.multiple_of(c * CHUNK, CHUNK)
        x32 = conv_silu(xs_ref, cwx_ref, cbx_ref, c, r0)
        xs_s[pl.ds(r0, CHUNK), :] = x32
        btc = conv_silu(b_ref, cwb_ref, cbb_ref, c, r0).T.astype(BF16)
        bt_s[c] = btc
        c_s[pl.ds(r0, CHUNK), :] = conv_silu(c_ref, cwc_ref, cbc_ref, c, r0).astype(BF16)

        dt = _softplus(dt_ref[pl.ds(r0, CHUNK), :] + dtb_ref[...])
        hi, mid, lo = _split3(dt * a_row)
        cums = _dot(tri2, hi) + _dot(tri2, mid) + _dot(tri2, lo)
        cum, rcum = cums[:CHUNK], cums[CHUNK:]
        cs = jnp.where(is_fwd_lane, cum, rcum)
        tot = jnp.where(is_fwd_lane[:1], cum[CHUNK - 1:CHUNK], rcum[0:1])
        cs_s[pl.ds(r0, CHUNK), :] = cs
        dts_s[pl.ds(r0, CHUNK), :] = dt
        dec_s[c] = expand(jnp.exp(jnp.broadcast_to(tot, (SUBLANES, LANES))))
        ew = expand(jnp.exp(tot - cs) * dt)
        sf_s[c] = _dot(btc, (x32 * ew[:, :GROUP_W]).astype(BF16))
        sb_s[c] = _dot(btc, (x32 * ew[:, GROUP_W:]).astype(BF16))
        return carry

    lax.fori_loop(0, nchunk, prep_body, 0, unroll=2)

    def scan(s_ref, lanes, reverse):
        state_s[...] = jnp.zeros_like(state_s)

        def body(i, carry):
            c = nchunk - 1 - i if reverse else i
            st = state_s[...]
            contrib = s_ref[c]
            s_ref[c] = st
            state_s[...] = st * dec_s[c][0:1, lanes] + contrib
            return carry

        lax.fori_loop(0, nchunk, body, 0)

    scan(sf_s, slice(0, GROUP_W), False)
    scan(sb_s, slice(GROUP_W, 2 * GROUP_W), True)

    dskip = dskip_ref[...]

    def out_body(c, carry):
        r0 = pl.multiple_of(c * CHUNK, CHUNK)
        dt = dts_s[pl.ds(r0, CHUNK), :]
        cs = cs_s[pl.ds(r0, CHUNK), :]
        u_t = (jnp.log(dt) - cs).T
        dsum = dt + pltpu.roll(dt, LANES - GROUP_HEADS, axis=1)
        d_t = jnp.log(dsum).T
        x32 = xs_s[pl.ds(r0, CHUNK), :]
        cc = c_s[pl.ds(r0, CHUNK), :]
        cb = _dot(cc, bt_s[c])
        ys = []
        for j in range(npair):
            xp = x32[:, j * LANES:(j + 1) * LANES]
            x_halves = (jnp.where(lo_half, xp, 0.0).astype(BF16), jnp.where(lo_half, 0.0, xp).astype(BF16))
            acc = None
            for par in range(2):
                r = 2 * j + par
                arg_f = cs[:, r:r + 1] + u_t[r:r + 1, :]
                arg_b = cs[:, GROUP_HEADS + r:GROUP_HEADS + r + 1] + u_t[GROUP_HEADS + r:GROUP_HEADS + r + 1, :]
                arg = jnp.where(lt, arg_f, jnp.where(gt, arg_b, d_t[r:r + 1, :]))
                mix = (cb * jnp.exp(arg)).astype(BF16)
                term = _dot(mix, x_halves[par])
                acc = term if acc is None else acc + term
            ys.append(acc)
        y = jnp.concatenate(ys, axis=1)
        ee = expand(jnp.exp(cs))
        y = y + _dot(cc, sf_s[c].astype(BF16)) * ee[:, :GROUP_W] + _dot(cc, sb_s[c].astype(BF16)) * ee[:, GROUP_W:]
        o_ref[pl.ds(r0, CHUNK), :] = (y + dskip * x32).astype(BF16)
        return carry

    lax.fori_loop(0, nchunk, out_body, 0, unroll=2)


def _ssd(proj, dt, cw, cb, dtb, alog, dskip, e_mat, batch, seq):
    n = batch * seq
    nchunk = seq // CHUNK
    wb = GROUP_W // LANES
    return pl.pallas_call(
        functools.partial(_ssd_kernel, seq=seq),
        grid=(batch, SSM_GROUPS),
        in_specs=[
            pl.BlockSpec((seq, GROUP_W), lambda b, g: (b, COL_XS // GROUP_W + g)),
            pl.BlockSpec((seq, LANES), lambda b, g: (b, COL_B // LANES + g)),
            pl.BlockSpec((seq, LANES), lambda b, g: (b, COL_C // LANES + g)),
            pl.BlockSpec((None, seq, LANES), lambda b, g: (g, b, 0)),
            pl.BlockSpec((SSM_CONV, GROUP_W), lambda b, g: (0, g)),
            pl.BlockSpec((SSM_CONV, LANES), lambda b, g: (0, SSM_GROUPS * wb + g)),
            pl.BlockSpec((SSM_CONV, LANES), lambda b, g: (0, SSM_GROUPS * wb + SSM_GROUPS + g)),
            pl.BlockSpec((1, GROUP_W), lambda b, g: (0, g)),
            pl.BlockSpec((1, LANES), lambda b, g: (0, SSM_GROUPS * wb + g)),
            pl.BlockSpec((1, LANES), lambda b, g: (0, SSM_GROUPS * wb + SSM_GROUPS + g)),
            pl.BlockSpec((None, 1, LANES), lambda b, g: (g, 0, 0)),
            pl.BlockSpec((None, 1, LANES), lambda b, g: (g, 0, 0)),
            pl.BlockSpec((1, GROUP_W), lambda b, g: (0, g)),
            pl.BlockSpec((LANES, 2 * GROUP_W), lambda b, g: (0, 0)),
        ],
        out_specs=pl.BlockSpec((seq, GROUP_W), lambda b, g: (b, g)),
        out_shape=jax.ShapeDtypeStruct((n, SSM_GROUPS * GROUP_W), BF16),
        scratch_shapes=[
            pltpu.VMEM((seq, GROUP_W), F32),
            pltpu.VMEM((nchunk, SSM_STATE, CHUNK), BF16),
            pltpu.VMEM((seq, SSM_STATE), BF16),
            pltpu.VMEM((seq, LANES), F32),
            pltpu.VMEM((seq, LANES), F32),
            pltpu.VMEM((nchunk, SUBLANES, 2 * GROUP_W), F32),
            pltpu.VMEM((nchunk, SSM_STATE, GROUP_W), F32),
            pltpu.VMEM((nchunk, SSM_STATE, GROUP_W), F32),
            pltpu.VMEM((SSM_STATE, GROUP_W), F32),
        ],
        compiler_params=_cparams("parallel", "parallel"),
        name="ssd",
    )(proj, proj, proj, dt, cw, cw, cw, cb, cb, cb, dtb, alog, dskip, e_mat)


def _attn_kernel(sink_ref, q_ref, k_ref, v_ref, bias_ref, o_ref, ke_s, ko_s, vt_s, *, seq):
    nblk = seq // QBLK
    kvw = KV_HEADS * HEAD_DIM
    ncol = kvw // LANES
    lane = lax.broadcasted_iota(jnp.int32, (QBLK, LANES), 1)
    lo_half = lane < HEAD_DIM

    zrow = jnp.zeros((WINDOW, KV_HEADS * LANES), BF16)
    for s in (ke_s, ko_s):
        s[0:WINDOW, :] = zrow
        s[WINDOW + seq:WINDOW + seq + WINDOW, :] = zrow
    zcol = jnp.zeros((kvw, WINDOW), BF16)
    vt_s[0] = zcol
    vt_s[nblk + 1] = zcol

    def fill_body(c, carry):
        r0 = pl.multiple_of(c * QBLK, QBLK)
        dst = pl.ds(WINDOW + r0, QBLK)
        kc = k_ref[pl.ds(r0, QBLK), :].astype(F32)
        for j in range(ncol):
            col = kc[:, j * LANES:(j + 1) * LANES]
            swp = pltpu.roll(col, HEAD_DIM, axis=1)
            g0, g1 = 2 * j, 2 * j + 1
            ke_s[dst, g0 * LANES:(g0 + 1) * LANES] = jnp.where(lo_half, col, 0.0).astype(BF16)
            ko_s[dst, g0 * LANES:(g0 + 1) * LANES] = jnp.where(lo_half, 0.0, swp).astype(BF16)
            ke_s[dst, g1 * LANES:(g1 + 1) * LANES] = jnp.where(lo_half, swp, 0.0).astype(BF16)
            ko_s[dst, g1 * LANES:(g1 + 1) * LANES] = jnp.where(lo_half, 0.0, col).astype(BF16)
        vt_s[c + 1] = v_ref[pl.ds(r0, QBLK), :].astype(F32).T.astype(BF16)
        return carry

    lax.fori_loop(0, nblk, fill_body, 0)

    def body(n, carry):
        r0 = pl.multiple_of(n * QBLK, QBLK)
        starts = (pl.multiple_of(jnp.where(n == 0, KEY_SPAN, 0), QBLK),
                  QBLK,
                  pl.multiple_of(jnp.where(n == nblk - 1, KEY_SPAN, KEY_SPAN - QBLK), QBLK))
        outs = []
        for pr in range(ATTN_HEADS // 2):
            qp = q_ref[pl.ds(r0, QBLK), pr * LANES:(pr + 1) * LANES]
            g = (2 * pr) // KV_REP
            vts = [vt_s[n + w, g * HEAD_DIM:(g + 1) * HEAD_DIM, :] for w in range(KEY_TILES)]
            for par, ks in ((0, ke_s), (1, ko_s)):
                h = 2 * pr + par
                sk = sink_ref[h]
                bias = jnp.concatenate([bias_ref[h, pl.ds(starts[w], QBLK), :] for w in range(KEY_TILES)], axis=0)
                t = _dot_nt(ks[pl.ds(r0, KEY_SPAN), g * LANES:(g + 1) * LANES], qp) + bias
                m = jnp.maximum(jnp.max(t, axis=0, keepdims=True), sk)
                p = jnp.exp(t - m)
                denom = jnp.sum(p, axis=0, keepdims=True) + jnp.exp(sk - m)
                pb = p.astype(BF16)
                pv = sum(_dot(vts[w], pb[w * QBLK:(w + 1) * QBLK]) for w in range(KEY_TILES))
                outs.append(pv * (1.0 / denom))
        o_ref[pl.ds(r0, QBLK), :] = jnp.concatenate(outs, axis=0).T.astype(BF16)
        return carry

    lax.fori_loop(0, nblk, body, 0)


def _attn(proj, bias_t, sink, batch, seq):
    n = batch * seq
    aw = ATTN_HEADS * HEAD_DIM
    kvw = KV_HEADS * HEAD_DIM
    return pl.pallas_call(
        functools.partial(_attn_kernel, seq=seq),
        grid=(batch,),
        in_specs=[
            pl.BlockSpec(memory_space=pltpu.SMEM),
            pl.BlockSpec((seq, aw), lambda b: (b, COL_Q // aw)),
            pl.BlockSpec((seq, kvw), lambda b: (b, COL_K // kvw)),
            pl.BlockSpec((seq, kvw), lambda b: (b, COL_V // kvw)),
            pl.BlockSpec((ATTN_HEADS, KEY_SPAN + QBLK, QBLK), lambda b: (0, 0, 0)),
        ],
        out_specs=pl.BlockSpec((seq, aw), lambda b: (b, 0)),
        out_shape=jax.ShapeDtypeStruct((n, aw), BF16),
        scratch_shapes=[
            pltpu.VMEM((seq + 2 * WINDOW, KV_HEADS * LANES), BF16),
            pltpu.VMEM((seq + 2 * WINDOW, KV_HEADS * LANES), BF16),
            pltpu.VMEM((seq // QBLK + 2, kvw, QBLK), BF16),
        ],
        compiler_params=_cparams("parallel"),
        name="attn",
    )(sink, proj, proj, proj, bias_t)


def _outproj_kernel(x_ref, ys_ref, z_ref, ya_ref, nw_ref, ws_ref, wa_ref, o_ref):
    z = z_ref[...].astype(F32)
    yg = ys_ref[...].astype(F32) * (z * _sigmoid(z))
    ysn = jnp.concatenate(
        [_rms(yg[:, g * GROUP_W:(g + 1) * GROUP_W], nw_ref[:, g * GROUP_W:(g + 1) * GROUP_W]).astype(BF16)
         for g in range(SSM_GROUPS)], axis=1)
    o_ref[...] = x_ref[...] + _dot(ysn, ws_ref[...]) + _dot(ya_ref[...], wa_ref[...])


def _outproj(x2, ys, proj, ya, nw, ws, wa, tm=1024):
    n = x2.shape[0]
    return pl.pallas_call(
        _outproj_kernel,
        grid=(n // tm,),
        in_specs=[
            pl.BlockSpec((tm, D_MODEL), lambda i: (i, 0)),
            pl.BlockSpec((tm, D_MODEL), lambda i: (i, 0)),
            pl.BlockSpec((tm, D_MODEL), lambda i: (i, COL_Z // D_MODEL)),
            pl.BlockSpec((tm, D_MODEL), lambda i: (i, 0)),
            pl.BlockSpec((1, D_MODEL), lambda i: (0, 0)),
            pl.BlockSpec((D_MODEL, D_MODEL), lambda i: (0, 0)),
            pl.BlockSpec((D_MODEL, D_MODEL), lambda i: (0, 0)),
        ],
        out_specs=pl.BlockSpec((tm, D_MODEL), lambda i: (i, 0)),
        out_shape=jax.ShapeDtypeStruct((n, D_MODEL), F32),
        compiler_params=_cparams("parallel"),
        name="outproj",
    )(x2, ys, proj, ya, nw, ws, wa)


def _ffn_kernel(x_ref, nw_ref, wg_ref, wu_ref, cw_ref, cb_ref, wd_ref, fw_ref, o_ref, h_ref, g_s, *, seq, final):
    j = pl.program_id(1)

    @pl.when(j == 0)
    def _():
        x = x_ref[...]
        h_ref[...] = _rms(x, nw_ref[...]).astype(BF16)
        o_ref[...] = x
        zpad = jnp.zeros((SUBLANES, FF_CHUNK), F32)
        g_s[0:SUBLANES, :] = zpad
        g_s[SUBLANES + seq:SUBLANES + seq + SUBLANES, :] = zpad

    h = h_ref[...]
    g_s[SUBLANES:SUBLANES + seq, :] = _dot(h, wg_ref[...])
    u = _dot(h, wu_ref[...])
    cw = cw_ref[...]
    gc = cb_ref[...] + g_s[SUBLANES:SUBLANES + seq, :] * cw[1:2]
    gc = gc + g_s[SUBLANES - 1:SUBLANES - 1 + seq, :] * cw[0:1] + g_s[SUBLANES + 1:SUBLANES + 1 + seq, :] * cw[2:3]
    act = (gc * _sigmoid(gc) * u).astype(BF16)
    o_ref[...] += _dot(act, wd_ref[...])

    if final:
        @pl.when(j == pl.num_programs(1) - 1)
        def _():
            o_ref[...] = _rms(o_ref[...], fw_ref[...])


def _ffn(x2, nw, wup, cw, cb, wd, fw, batch, seq, final):
    n = batch * seq
    nck = D_FF // FF_CHUNK
    return pl.pallas_call(
        functools.partial(_ffn_kernel, seq=seq, final=final),
        grid=(batch, nck),
        in_specs=[
            pl.BlockSpec((seq, D_MODEL), lambda b, j: (b, 0)),
            pl.BlockSpec((1, D_MODEL), lambda b, j: (0, 0)),
            pl.BlockSpec((None, D_MODEL, FF_CHUNK), lambda b, j: (j, 0, 0)),
            pl.BlockSpec((None, D_MODEL, FF_CHUNK), lambda b, j: (nck + j, 0, 0)),
            pl.BlockSpec((FFN_CONV, FF_CHUNK), lambda b, j: (0, j)),
            pl.BlockSpec((1, FF_CHUNK), lambda b, j: (0, j)),
            pl.BlockSpec((FF_CHUNK, D_MODEL), lambda b, j: (j, 0)),
            pl.BlockSpec((1, D_MODEL), lambda b, j: (0, 0)),
        ],
        out_specs=pl.BlockSpec((seq, D_MODEL), lambda b, j: (b, 0)),
        out_shape=jax.ShapeDtypeStruct((n, D_MODEL), F32),
        scratch_shapes=[pltpu.VMEM((seq, D_MODEL), BF16),
                        pltpu.VMEM((seq + 2 * SUBLANES, FF_CHUNK), F32)],
        compiler_params=_cparams("parallel", "arbitrary"),
        name="ffn_final" if final else "ffn",
    )(x2, nw, wup, wup, cw, cb, wd, fw)


def _t5_bucket(rel):
    half = REL_BUCKETS // 2
    max_exact = half // 2
    ret = jnp.where(rel > 0, half, 0)
    n = jnp.abs(rel)
    nf = jnp.maximum(n, 1).astype(F32)
    large = max_exact + (jnp.log(nf / max_exact) / math.log(REL_MAX_DIST / max_exact)
                         * (half - max_exact)).astype(jnp.int32)
    large = jnp.minimum(large, half - 1)
    return ret + jnp.where(n < max_exact, n, large)


def _band_bias_t(rel_bias):
    rel = jnp.arange(KEY_SPAN)[:, None] - WINDOW - jnp.arange(QBLK)[None, :]
    bias = rel_bias.astype(F32)[_t5_bucket(rel)].transpose(2, 0, 1)
    bias = jnp.where((jnp.abs(rel) <= WINDOW)[None], bias, -jnp.inf)
    return jnp.concatenate([bias, jnp.full((ATTN_HEADS, QBLK, QBLK), -jnp.inf, F32)], axis=1)


def _head_lanes(p):
    q = p.astype(F32).reshape(2, SSM_GROUPS, GROUP_HEADS).transpose(1, 0, 2).reshape(SSM_GROUPS, 2 * GROUP_HEADS)
    return jnp.pad(q, ((0, 0), (0, LANES - 2 * GROUP_HEADS)))[:, None, :]


def _expand_matrix():
    j = jnp.arange(LANES)[:, None]
    col = jnp.arange(2 * GROUP_W)[None, :]
    return jnp.where((j < 2 * GROUP_HEADS) & (col // HEAD_DIM == j), 1.0, 0.0).astype(BF16)


def kernel(x, rel_bias, norm1_w, w_in, conv_w, conv_b, dt_bias, a_log, d_skip, ssm_norm_w, attn_sink,
           w_out, norm2_w, w_up, ffn_conv_w, ffn_conv_b, w_down, final_norm_w):
    batch, seq, _ = x.shape
    assert seq % CHUNK == 0 and seq % QBLK == 0
    n = batch * seq
    depth = w_in.shape[0]
    x2 = x.reshape(n, D_MODEL)
    bias_t = _band_bias_t(rel_bias)
    e_mat = _expand_matrix()
    zw, xw_end = 1024, 2560
    dt_end = xw_end + 2 * SSM_HEADS
    q_end = dt_end + ATTN_HEADS * HEAD_DIM
    scale = HEAD_DIM ** -0.5
    nck = D_FF // FF_CHUNK

    for i in range(depth):
        wi = w_in[i]
        w_main = jnp.concatenate(
            [wi[:, :zw], wi[:, dt_end:q_end] * scale, wi[:, zw:xw_end], wi[:, q_end:]], axis=1).astype(BF16)
        wdt = wi[:, xw_end:dt_end].reshape(D_MODEL, 2, SSM_GROUPS, GROUP_HEADS).transpose(2, 0, 1, 3)
        wdt = jnp.pad(wdt.reshape(SSM_GROUPS, D_MODEL, 2 * GROUP_HEADS),
                      ((0, 0), (0, 0), (0, LANES - 2 * GROUP_HEADS))).astype(BF16)
        proj, dt = _inproj(x2, norm1_w[i][None], w_main, wdt)
        y_ssm = _ssd(proj, dt, conv_w[i], conv_b[i][None], _head_lanes(dt_bias[i]), _head_lanes(a_log[i]),
                     jnp.repeat(d_skip[i].astype(F32), HEAD_DIM)[None], e_mat, batch, seq)
        y_attn = _attn(proj, bias_t, attn_sink[i].astype(F32), batch, seq)
        wo = w_out[i].astype(BF16)
        x2 = _outproj(x2, y_ssm, proj, y_attn, ssm_norm_w[i][None], wo[:D_MODEL], wo[D_MODEL:])
        wup = w_up[i].astype(BF16).reshape(D_MODEL, 2 * nck, FF_CHUNK).transpose(1, 0, 2)
        x2 = _ffn(x2, norm2_w[i][None], wup, ffn_conv_w[i], ffn_conv_b[i][None],
                  w_down[i].astype(BF16), final_norm_w[None], batch, seq, final=(i == depth - 1))

    return x2.reshape(batch, seq, D_MODEL)
```

```python
import functools
import math

import jax
import jax.numpy as jnp
from jax import lax
from jax.experimental import pallas as pl
from jax.experimental.pallas import tpu as pltpu

F32 = jnp.float32
BF16 = jnp.bfloat16

D_MODEL = 1024
HEAD_DIM = 64
SSM_HEADS = 16
SSM_GROUPS = 2
GROUP_HEADS = SSM_HEADS // SSM_GROUPS
GROUP_W = GROUP_HEADS * HEAD_DIM
SSM_STATE = 128
SSM_CONV = 7
CHUNK = 128
ATTN_HEADS = 16
KV_HEADS = 4
KV_REP = ATTN_HEADS // KV_HEADS
WINDOW = 128
QBLK = 128
KEY_SPAN = QBLK + 2 * WINDOW
KEY_TILES = KEY_SPAN // QBLK
REL_BUCKETS = 32
REL_MAX_DIST = 128
D_FF = 2816
FFN_CONV = 3
FF_CHUNK = 256
EPS = 1e-6
LOG2E = math.log2(math.e)
LANES = 128
SUBLANES = 8
HALO_ROWS = 2 * SUBLANES

COL_Z, COL_Q, COL_XS, COL_B, COL_C, COL_K, COL_V = 0, 1024, 2048, 3072, 3328, 3584, 3840
PROJ_COLS = 4096

VMEM_LIMIT = 56 * 1024 * 1024


def _cparams(*sem):
    return pltpu.CompilerParams(dimension_semantics=sem, vmem_limit_bytes=VMEM_LIMIT)


def _sigmoid(x):
    return 1.0 / (1.0 + jnp.exp(-x))


def _softplus(x):
    return jnp.maximum(x, 0.0) + jnp.log1p(jnp.exp(-jnp.abs(x)))


def _rms(x, w):
    ms = jnp.mean(x * x, axis=-1, keepdims=True)
    return x * lax.rsqrt(ms + EPS) * w


def _split3(x):
    hi = x.astype(BF16)
    r1 = x - hi.astype(F32)
    mid = r1.astype(BF16)
    lo = (r1 - mid.astype(F32)).astype(BF16)
    return hi, mid, lo


def _dot(a, b):
    return jnp.dot(a, b, preferred_element_type=F32)


def _dot_nt(a, b):
    return lax.dot_general(a, b, (((1,), (1,)), ((), ())), preferred_element_type=F32)


def _inproj_kernel(x_ref, nw_ref, w_ref, wdt_ref, proj_ref, dt_ref, *, tn):
    hb = _rms(x_ref[...], nw_ref[...]).astype(BF16)
    for g in range(SSM_GROUPS):
        dt_ref[g] = _dot(hb, wdt_ref[g])
    for j in range(PROJ_COLS // tn):
        proj_ref[:, j * tn:(j + 1) * tn] = _dot(hb, w_ref[:, j * tn:(j + 1) * tn]).astype(BF16)


def _inproj(x2, nw, w, wdt, tm=1024, tn=1024):
    n = x2.shape[0]
    return pl.pallas_call(
        functools.partial(_inproj_kernel, tn=tn),
        grid=(n // tm,),
        in_specs=[
            pl.BlockSpec((tm, D_MODEL), lambda i: (i, 0)),
            pl.BlockSpec((1, D_MODEL), lambda i: (0, 0)),
            pl.BlockSpec((D_MODEL, PROJ_COLS), lambda i: (0, 0)),
            pl.BlockSpec((SSM_GROUPS, D_MODEL, LANES), lambda i: (0, 0, 0)),
        ],
        out_specs=[
            pl.BlockSpec((tm, PROJ_COLS), lambda i: (i, 0)),
            pl.BlockSpec((SSM_GROUPS, tm, LANES), lambda i: (0, i, 0)),
        ],
        out_shape=[
            jax.ShapeDtypeStruct((n, PROJ_COLS), BF16),
            jax.ShapeDtypeStruct((SSM_GROUPS, n, LANES), F32),
        ],
        compiler_params=_cparams("parallel"),
        name="inproj",
    )(x2, nw, w, wdt)


def _ssd_kernel(xs_ref, b_ref, c_ref, dt_ref, cwx_ref, cwb_ref, cwc_ref, cbx_ref, cbb_ref,
                cbc_ref, dtb_ref, alog_ref, dskip_ref, e_ref, o_ref,
                xs_s, bt_s, c_s, cs_s, ut_s, cb_s, dec_s, sf_s, sb_s, state_s, ew_r, mix_r, *, seq):
    nchunk = seq // CHUNK
    npair = GROUP_W // LANES
    lane = lax.broadcasted_iota(jnp.int32, (CHUNK, LANES), 1)
    row = lax.broadcasted_iota(jnp.int32, (CHUNK, LANES), 0)
    is_fwd_lane = lane < GROUP_HEADS
    lo_half = lane < HEAD_DIM
    lt = lane < row
    gt = lane > row
    row2 = lax.broadcasted_iota(jnp.int32, (2 * CHUNK, LANES), 0)
    lane2 = lax.broadcasted_iota(jnp.int32, (2 * CHUNK, LANES), 1)
    tri2 = jnp.where(row2 < CHUNK, jnp.where(lane2 <= row2, 1.0, 0.0),
                     jnp.where(lane2 >= row2 - CHUNK, 1.0, 0.0)).astype(BF16)

    half = SSM_CONV // 2
    win_rows = CHUNK + 2 * HALO_ROWS
    srow = lax.broadcasted_iota(jnp.int32, (half * CHUNK, win_rows), 0)
    scol = lax.broadcasted_iota(jnp.int32, (half * CHUNK, win_rows), 1)
    sblk = jnp.right_shift(srow, CHUNK.bit_length() - 1)
    shift_mat = jnp.where(scol == srow - sblk * CHUNK + (HALO_ROWS - half) + sblk, 1.0, 0.0).astype(BF16)

    def conv_silu(src, w_ref, bias_ref, c, r0):
        lo = pl.multiple_of(jnp.maximum(r0 - HALO_ROWS, 0), HALO_ROWS)
        hi = pl.multiple_of(jnp.minimum(r0 + CHUNK, seq - HALO_ROWS), HALO_ROWS)
        prev = jnp.where(c > 0, src[pl.ds(lo, HALO_ROWS), :].astype(F32), 0.0)
        nxt = jnp.where(c < nchunk - 1, src[pl.ds(hi, HALO_ROWS), :].astype(F32), 0.0)
        cur = src[pl.ds(r0, CHUNK), :]
        curf = cur.astype(F32)
        win = jnp.concatenate([prev.astype(BF16), cur, nxt.astype(BF16)], axis=0)
        before = _dot(shift_mat, win)
        tail = jnp.concatenate([curf, nxt], axis=0)
        acc = bias_ref[...] + curf * w_ref[half:half + 1, :]
        for k in range(half):
            acc = acc + before[k * CHUNK:(k + 1) * CHUNK] * w_ref[k:k + 1, :]
        for k in range(half + 1, SSM_CONV):
            acc = acc + tail[k - half:k - half + CHUNK] * w_ref[k:k + 1, :]
        return acc * _sigmoid(acc)

    a2_row = -jnp.exp(alog_ref[...]) * LOG2E
    e_mat = e_ref[...]

    def expand(q, exact=False):
        hi = q.astype(BF16)
        out = _dot(hi, e_mat)
        if exact:
            out = out + _dot((q - hi.astype(F32)).astype(BF16), e_mat)
        return out

    xs_s[0:CHUNK, :] = jnp.zeros((CHUNK, GROUP_W), F32)
    bt_s[0] = jnp.zeros((SSM_STATE, CHUNK), BF16)
    c_s[0:CHUNK, :] = jnp.zeros((CHUNK, SSM_STATE), BF16)
    ew_r[...] = jnp.zeros_like(ew_r)

    def prep_body(t, carry):
        cf = jnp.maximum(t - 1, 0)
        rf = pl.multiple_of(cf * CHUNK, CHUNK)
        ewf = ew_r[(t + 1) % 2]
        xf = xs_s[pl.ds(rf, CHUNK), :]
        btf = bt_s[cf]
        sf_s[cf] = _dot(btf, (xf * ewf[:, :GROUP_W]).astype(BF16))
        sb_s[cf] = _dot(btf, (xf * ewf[:, GROUP_W:]).astype(BF16))
        cb_s[cf] = _dot(c_s[pl.ds(rf, CHUNK), :], btf)

        c = jnp.minimum(t, nchunk - 1)
        r0 = pl.multiple_of(c * CHUNK, CHUNK)
        xs_s[pl.ds(r0, CHUNK), :] = conv_silu(xs_ref, cwx_ref, cbx_ref, c, r0)
        bt_s[c] = conv_silu(b_ref, cwb_ref, cbb_ref, c, r0).T.astype(BF16)
        c_s[pl.ds(r0, CHUNK), :] = conv_silu(c_ref, cwc_ref, cbc_ref, c, r0).astype(BF16)

        dt = _softplus(dt_ref[pl.ds(r0, CHUNK), :] + dtb_ref[...])
        hi, mid, lo = _split3(dt * a2_row)
        cums = _dot(tri2, hi) + _dot(tri2, mid) + _dot(tri2, lo)
        cum, rcum = cums[:CHUNK], cums[CHUNK:]
        cs = jnp.where(is_fwd_lane, cum, rcum)
        tot = jnp.where(is_fwd_lane[:1], cum[CHUNK - 1:CHUNK], rcum[0:1])
        cs_s[pl.ds(r0, CHUNK), :] = cs
        dsum = dt + pltpu.roll(dt, LANES - GROUP_HEADS, axis=1)
        ut_s[c, 0:2 * GROUP_HEADS, :] = (jnp.log2(dt) - cs).T[0:2 * GROUP_HEADS]
        ut_s[c, 2 * GROUP_HEADS:3 * GROUP_HEADS, :] = jnp.log2(dsum).T[0:GROUP_HEADS]
        dec_s[c] = expand(jnp.exp2(jnp.broadcast_to(tot, (SUBLANES, LANES))), exact=True)
        ew_r[t % 2] = expand(jnp.exp2(tot - cs) * dt)
        return carry

    lax.fori_loop(0, nchunk + 1, prep_body, 0)

    def scan(s_ref, lanes, reverse):
        state_s[...] = jnp.zeros_like(state_s)

        def body(i, carry):
            c = nchunk - 1 - i if reverse else i
            st = state_s[...]
            contrib = s_ref[c]
            s_ref[c] = st
            state_s[...] = st * dec_s[c][0:1, lanes] + contrib
            return carry

        lax.fori_loop(0, nchunk, body, 0)

    scan(sf_s, slice(0, GROUP_W), False)
    scan(sb_s, slice(GROUP_W, 2 * GROUP_W), True)

    dskip = dskip_ref[...]
    mix_r[...] = jnp.zeros_like(mix_r)

    def out_body(c, carry):
        cf = jnp.maximum(c - 1, 0)
        rf = pl.multiple_of(cf * CHUNK, CHUNK)
        slot = (c + 1) % 2
        x32 = xs_s[pl.ds(rf, CHUNK), :]
        cc = c_s[pl.ds(rf, CHUNK), :]
        ys = []
        for j in range(npair):
            xp = x32[:, j * LANES:(j + 1) * LANES]
            ys.append(_dot(mix_r[slot, 2 * j], jnp.where(lo_half, xp, 0.0).astype(BF16))
                      + _dot(mix_r[slot, 2 * j + 1], jnp.where(lo_half, 0.0, xp).astype(BF16)))
        y = jnp.concatenate(ys, axis=1)
        ee = expand(jnp.exp2(cs_s[pl.ds(rf, CHUNK), :]))
        y = y + _dot(cc, sf_s[cf].astype(BF16)) * ee[:, :GROUP_W] + _dot(cc, sb_s[cf].astype(BF16)) * ee[:, GROUP_W:]
        o_ref[pl.ds(rf, CHUNK), :] = (y + dskip * x32).astype(BF16)

        cn = jnp.minimum(c, nchunk - 1)
        r0 = pl.multiple_of(cn * CHUNK, CHUNK)
        cs = cs_s[pl.ds(r0, CHUNK), :]
        cb = cb_s[cn]
        for r in range(GROUP_HEADS):
            rb = GROUP_HEADS + r
            arg_f = cs[:, r:r + 1] + ut_s[cn, r:r + 1, :]
            arg_b = cs[:, rb:rb + 1] + ut_s[cn, rb:rb + 1, :]
            arg = jnp.where(lt, arg_f, jnp.where(gt, arg_b, ut_s[cn, rb + GROUP_HEADS:rb + GROUP_HEADS + 1, :]))
            mix_r[c % 2, r] = (cb * jnp.exp2(arg)).astype(BF16)
        return carry

    lax.fori_loop(0, nchunk + 1, out_body, 0)


def _ssd(proj, dt, cw, cb, dtb, alog, dskip, e_mat, batch, seq):
    n = batch * seq
    nchunk = seq // CHUNK
    wb = GROUP_W // LANES
    return pl.pallas_call(
        functools.partial(_ssd_kernel, seq=seq),
        grid=(batch, SSM_GROUPS),
        in_specs=[
            pl.BlockSpec((seq, GROUP_W), lambda b, g: (b, COL_XS // GROUP_W + g)),
            pl.BlockSpec((seq, LANES), lambda b, g: (b, COL_B // LANES + g)),
            pl.BlockSpec((seq, LANES), lambda b, g: (b, COL_C // LANES + g)),
            pl.BlockSpec((None, seq, LANES), lambda b, g: (g, b, 0)),
            pl.BlockSpec((SSM_CONV, GROUP_W), lambda b, g: (0, g)),
            pl.BlockSpec((SSM_CONV, LANES), lambda b, g: (0, SSM_GROUPS * wb + g)),
            pl.BlockSpec((SSM_CONV, LANES), lambda b, g: (0, SSM_GROUPS * wb + SSM_GROUPS + g)),
            pl.BlockSpec((1, GROUP_W), lambda b, g: (0, g)),
            pl.BlockSpec((1, LANES), lambda b, g: (0, SSM_GROUPS * wb + g)),
            pl.BlockSpec((1, LANES), lambda b, g: (0, SSM_GROUPS * wb + SSM_GROUPS + g)),
            pl.BlockSpec((None, 1, LANES), lambda b, g: (g, 0, 0)),
            pl.BlockSpec((None, 1, LANES), lambda b, g: (g, 0, 0)),
            pl.BlockSpec((1, GROUP_W), lambda b, g: (0, g)),
            pl.BlockSpec((LANES, 2 * GROUP_W), lambda b, g: (0, 0)),
        ],
        out_specs=pl.BlockSpec((seq, GROUP_W), lambda b, g: (b, g)),
        out_shape=jax.ShapeDtypeStruct((n, SSM_GROUPS * GROUP_W), BF16),
        scratch_shapes=[
            pltpu.VMEM((seq, GROUP_W), F32),
            pltpu.VMEM((nchunk, SSM_STATE, CHUNK), BF16),
            pltpu.VMEM((seq, SSM_STATE), BF16),
            pltpu.VMEM((seq, LANES), F32),
            pltpu.VMEM((nchunk, 3 * GROUP_HEADS, LANES), F32),
            pltpu.VMEM((nchunk, CHUNK, CHUNK), F32),
            pltpu.VMEM((nchunk, SUBLANES, 2 * GROUP_W), F32),
            pltpu.VMEM((nchunk, SSM_STATE, GROUP_W), F32),
            pltpu.VMEM((nchunk, SSM_STATE, GROUP_W), F32),
            pltpu.VMEM((SSM_STATE, GROUP_W), F32),
            pltpu.VMEM((2, CHUNK, 2 * GROUP_W), F32),
            pltpu.VMEM((2, GROUP_HEADS, CHUNK, CHUNK), BF16),
        ],
        compiler_params=_cparams("parallel", "parallel"),
        name="ssd",
    )(proj, proj, proj, dt, cw, cw, cw, cb, cb, cb, dtb, alog, dskip, e_mat)


def _attn_kernel(sink_ref, q_ref, k_ref, v_ref, bias_ref, o_ref, ke_s, ko_s, vt_s, *, seq):
    nblk = seq // QBLK
    kvw = KV_HEADS * HEAD_DIM
    ncol = kvw // LANES
    lane = lax.broadcasted_iota(jnp.int32, (QBLK, LANES), 1)
    lo_half = lane < HEAD_DIM

    zrow = jnp.zeros((WINDOW, KV_HEADS * LANES), BF16)
    for s in (ke_s, ko_s):
        s[0:WINDOW, :] = zrow
        s[WINDOW + seq:WINDOW + seq + WINDOW, :] = zrow
    zcol = jnp.zeros((kvw, WINDOW), BF16)
    vt_s[0] = zcol
    vt_s[nblk + 1] = zcol

    def fill_body(c, carry):
        r0 = pl.multiple_of(c * QBLK, QBLK)
        dst = pl.ds(WINDOW + r0, QBLK)
        kc = k_ref[pl.ds(r0, QBLK), :].astype(F32)
        for j in range(ncol):
            col = kc[:, j * LANES:(j + 1) * LANES]
            swp = pltpu.roll(col, HEAD_DIM, axis=1)
            g0, g1 = 2 * j, 2 * j + 1
            ke_s[dst, g0 * LANES:(g0 + 1) * LANES] = jnp.where(lo_half, col, 0.0).astype(BF16)
            ko_s[dst, g0 * LANES:(g0 + 1) * LANES] = jnp.where(lo_half, 0.0, swp).astype(BF16)
            ke_s[dst, g1 * LANES:(g1 + 1) * LANES] = jnp.where(lo_half, swp, 0.0).astype(BF16)
            ko_s[dst, g1 * LANES:(g1 + 1) * LANES] = jnp.where(lo_half, 0.0, col).astype(BF16)
        vt_s[c + 1] = v_ref[pl.ds(r0, QBLK), :].astype(F32).T.astype(BF16)
        return carry

    lax.fori_loop(0, nblk, fill_body, 0)

    def body(n, carry):
        r0 = pl.multiple_of(n * QBLK, QBLK)
        starts = (pl.multiple_of(jnp.where(n == 0, KEY_SPAN, 0), QBLK),
                  QBLK,
                  pl.multiple_of(jnp.where(n == nblk - 1, KEY_SPAN, KEY_SPAN - QBLK), QBLK))
        outs = []
        for pr in range(ATTN_HEADS // 2):
            qp = q_ref[pl.ds(r0, QBLK), pr * LANES:(pr + 1) * LANES]
            g = (2 * pr) // KV_REP
            vts = [vt_s[n + w, g * HEAD_DIM:(g + 1) * HEAD_DIM, :] for w in range(KEY_TILES)]
            for par, ks in ((0, ke_s), (1, ko_s)):
                h = 2 * pr + par
                sk = sink_ref[h]
                bias = jnp.concatenate([bias_ref[h, pl.ds(starts[w], QBLK), :] for w in range(KEY_TILES)], axis=0)
                t = _dot_nt(ks[pl.ds(r0, KEY_SPAN), g * LANES:(g + 1) * LANES], qp) + bias
                m = jnp.maximum(jnp.max(t, axis=0, keepdims=True), sk)
                p = jnp.exp(t - m)
                denom = jnp.sum(p, axis=0, keepdims=True) + jnp.exp(sk - m)
                pb = p.astype(BF16)
                pv = sum(_dot(vts[w], pb[w * QBLK:(w + 1) * QBLK]) for w in range(KEY_TILES))
                outs.append(pv * (1.0 / denom))
        o_ref[pl.ds(r0, QBLK), :] = jnp.concatenate(outs, axis=0).T.astype(BF16)
        return carry

    lax.fori_loop(0, nblk, body, 0)


def _attn(proj, bias_t, sink, batch, seq):
    n = batch * seq
    aw = ATTN_HEADS * HEAD_DIM
    kvw = KV_HEADS * HEAD_DIM
    return pl.pallas_call(
        functools.partial(_attn_kernel, seq=seq),
        grid=(batch,),
        in_specs=[
            pl.BlockSpec(memory_space=pltpu.SMEM),
            pl.BlockSpec((seq, aw), lambda b: (b, COL_Q // aw)),
            pl.BlockSpec((seq, kvw), lambda b: (b, COL_K // kvw)),
            pl.BlockSpec((seq, kvw), lambda b: (b, COL_V // kvw)),
            pl.BlockSpec((ATTN_HEADS, KEY_SPAN + QBLK, QBLK), lambda b: (0, 0, 0)),
        ],
        out_specs=pl.BlockSpec((seq, aw), lambda b: (b, 0)),
        out_shape=jax.ShapeDtypeStruct((n, aw), BF16),
        scratch_shapes=[
            pltpu.VMEM((seq + 2 * WINDOW, KV_HEADS * LANES), BF16),
            pltpu.VMEM((seq + 2 * WINDOW, KV_HEADS * LANES), BF16),
            pltpu.VMEM((seq // QBLK + 2, kvw, QBLK), BF16),
        ],
        compiler_params=_cparams("parallel"),
        name="attn",
    )(sink, proj, proj, proj, bias_t)


def _outproj_kernel(x_ref, ys_ref, z_ref, ya_ref, nw_ref, ws_ref, wa_ref, o_ref):
    z = z_ref[...].astype(F32)
    yg = ys_ref[...].astype(F32) * (z * _sigmoid(z))
    ysn = jnp.concatenate(
        [_rms(yg[:, g * GROUP_W:(g + 1) * GROUP_W], nw_ref[:, g * GROUP_W:(g + 1) * GROUP_W]).astype(BF16)
         for g in range(SSM_GROUPS)], axis=1)
    o_ref[...] = x_ref[...] + _dot(ysn, ws_ref[...]) + _dot(ya_ref[...], wa_ref[...])


def _outproj(x2, ys, proj, ya, nw, ws, wa, tm=1024):
    n = x2.shape[0]
    return pl.pallas_call(
        _outproj_kernel,
        grid=(n // tm,),
        in_specs=[
            pl.BlockSpec((tm, D_MODEL), lambda i: (i, 0)),
            pl.BlockSpec((tm, D_MODEL), lambda i: (i, 0)),
            pl.BlockSpec((tm, D_MODEL), lambda i: (i, COL_Z // D_MODEL)),
            pl.BlockSpec((tm, D_MODEL), lambda i: (i, 0)),
            pl.BlockSpec((1, D_MODEL), lambda i: (0, 0)),
            pl.BlockSpec((D_MODEL, D_MODEL), lambda i: (0, 0)),
            pl.BlockSpec((D_MODEL, D_MODEL), lambda i: (0, 0)),
        ],
        out_specs=pl.BlockSpec((tm, D_MODEL), lambda i: (i, 0)),
        out_shape=jax.ShapeDtypeStruct((n, D_MODEL), F32),
        compiler_params=_cparams("parallel"),
        name="outproj",
    )(x2, ys, proj, ya, nw, ws, wa)


def _ffn_kernel(x_ref, nw_ref, wg_ref, wu_ref, cw_ref, cb_ref, wd_ref, fw_ref, o_ref, h_ref, *, seq, final):
    j = pl.program_id(1)

    @pl.when(j == 0)
    def _():
        x = x_ref[...]
        h_ref[...] = _rms(x, nw_ref[...]).astype(BF16)
        o_ref[...] = x

    h = h_ref[...]
    g = _dot(h, wg_ref[...])
    u = _dot(h, wu_ref[...])
    cw = cw_ref[...]
    zpad = jnp.zeros((SUBLANES, FF_CHUNK), F32)
    gp = jnp.concatenate([zpad, g, zpad], axis=0)
    gc = cb_ref[...] + g * cw[1:2]
    gc = gc + gp[SUBLANES - 1:SUBLANES - 1 + seq] * cw[0:1] + gp[SUBLANES + 1:SUBLANES + 1 + seq] * cw[2:3]
    act = (gc * _sigmoid(gc) * u).astype(BF16)
    o_ref[...] += _dot(act, wd_ref[...])

    if final:
        @pl.when(j == pl.num_programs(1) - 1)
        def _():
            o_ref[...] = _rms(o_ref[...], fw_ref[...])


def _ffn(x2, nw, wup, cw, cb, wd, fw, batch, seq, final):
    n = batch * seq
    nck = D_FF // FF_CHUNK
    return pl.pallas_call(
        functools.partial(_ffn_kernel, seq=seq, final=final),
        grid=(batch, nck),
        in_specs=[
            pl.BlockSpec((seq, D_MODEL), lambda b, j: (b, 0)),
            pl.BlockSpec((1, D_MODEL), lambda b, j: (0, 0)),
            pl.BlockSpec((None, D_MODEL, FF_CHUNK), lambda b, j: (j, 0, 0)),
            pl.BlockSpec((None, D_MODEL, FF_CHUNK), lambda b, j: (nck + j, 0, 0)),
            pl.BlockSpec((FFN_CONV, FF_CHUNK), lambda b, j: (0, j)),
            pl.BlockSpec((1, FF_CHUNK), lambda b, j: (0, j)),
            pl.BlockSpec((FF_CHUNK, D_MODEL), lambda b, j: (j, 0)),
            pl.BlockSpec((1, D_MODEL), lambda b, j: (0, 0)),
        ],
        out_specs=pl.BlockSpec((seq, D_MODEL), lambda b, j: (b, 0)),
        out_shape=jax.ShapeDtypeStruct((n, D_MODEL), F32),
        scratch_shapes=[pltpu.VMEM((seq, D_MODEL), BF16)],
        compiler_params=_cparams("parallel", "arbitrary"),
        name="ffn_final" if final else "ffn",
    )(x2, nw, wup, wup, cw, cb, wd, fw)


def _t5_bucket(rel):
    half = REL_BUCKETS // 2
    max_exact = half // 2
    ret = jnp.where(rel > 0, half, 0)
    n = jnp.abs(rel)
    nf = jnp.maximum(n, 1).astype(F32)
    large = max_exact + (jnp.log(nf / max_exact) / math.log(REL_MAX_DIST / max_exact)
                         * (half - max_exact)).astype(jnp.int32)
    large = jnp.minimum(large, half - 1)
    return ret + jnp.where(n < max_exact, n, large)


def _band_bias_t(rel_bias):
    rel = jnp.arange(KEY_SPAN)[:, None] - WINDOW - jnp.arange(QBLK)[None, :]
    onehot = (_t5_bucket(rel)[..., None] == jnp.arange(REL_BUCKETS)).astype(F32)
    bias = jnp.einsum("jib,bh->hji", onehot, rel_bias.astype(F32), precision=lax.Precision.HIGHEST)
    bias = jnp.where((jnp.abs(rel) <= WINDOW)[None], bias, -jnp.inf)
    return jnp.concatenate([bias, jnp.full((ATTN_HEADS, QBLK, QBLK), -jnp.inf, F32)], axis=1)


def _head_lanes(p):
    q = p.astype(F32).reshape(2, SSM_GROUPS, GROUP_HEADS).transpose(1, 0, 2).reshape(SSM_GROUPS, 2 * GROUP_HEADS)
    return jnp.pad(q, ((0, 0), (0, LANES - 2 * GROUP_HEADS)))[:, None, :]


def _expand_matrix():
    j = jnp.arange(LANES)[:, None]
    col = jnp.arange(2 * GROUP_W)[None, :]
    return jnp.where((j < 2 * GROUP_HEADS) & (col // HEAD_DIM == j), 1.0, 0.0).astype(BF16)


def kernel(x, rel_bias, norm1_w, w_in, conv_w, conv_b, dt_bias, a_log, d_skip, ssm_norm_w, attn_sink,
           w_out, norm2_w, w_up, ffn_conv_w, ffn_conv_b, w_down, final_norm_w):
    batch, seq, _ = x.shape
    assert seq % CHUNK == 0 and seq % QBLK == 0
    n = batch * seq
    depth = w_in.shape[0]
    x2 = x.reshape(n, D_MODEL)
    bias_t = _band_bias_t(rel_bias)
    e_mat = _expand_matrix()
    zw, xw_end = 1024, 2560
    dt_end = xw_end + 2 * SSM_HEADS
    q_end = dt_end + ATTN_HEADS * HEAD_DIM
    scale = HEAD_DIM ** -0.5
    nck = D_FF // FF_CHUNK

    for i in range(depth):
        wi = w_in[i]
        w_main = jnp.concatenate(
            [wi[:, :zw], wi[:, dt_end:q_end] * scale, wi[:, zw:xw_end], wi[:, q_end:]], axis=1).astype(BF16)
        wdt = wi[:, xw_end:dt_end].reshape(D_MODEL, 2, SSM_GROUPS, GROUP_HEADS).transpose(2, 0, 1, 3)
        wdt = jnp.pad(wdt.reshape(SSM_GROUPS, D_MODEL, 2 * GROUP_HEADS),
                      ((0, 0), (0, 0), (0, LANES - 2 * GROUP_HEADS))).astype(BF16)
        proj, dt = _inproj(x2, norm1_w[i][None], w_main, wdt)
        y_ssm = _ssd(proj, dt, conv_w[i], conv_b[i][None], _head_lanes(dt_bias[i]), _head_lanes(a_log[i]),
                     jnp.repeat(d_skip[i].astype(F32), HEAD_DIM)[None], e_mat, batch, seq)
        y_attn = _attn(proj, bias_t, attn_sink[i].astype(F32), batch, seq)
        wo = w_out[i].astype(BF16)
        x2 = _outproj(x2, y_ssm, proj, y_attn, ssm_norm_w[i][None], wo[:D_MODEL], wo[D_MODEL:])
        wup = w_up[i].astype(BF16).reshape(D_MODEL, 2 * nck, FF_CHUNK).transpose(1, 0, 2)
        x2 = _ffn(x2, norm2_w[i][None], wup, ffn_conv_w[i], ffn_conv_b[i][None],
                  w_down[i].astype(BF16), final_norm_w[None], batch, seq, final=(i == depth - 1))

    return x2.reshape(batch, seq, D_MODEL)
```

```python
import functools
import math

import jax
import jax.numpy as jnp
from jax import lax
from jax.experimental import pallas as pl
from jax.experimental.pallas import tpu as pltpu

F32 = jnp.float32
BF16 = jnp.bfloat16

D_MODEL = 1024
HEAD_DIM = 64
SSM_HEADS = 16
SSM_GROUPS = 2
GROUP_HEADS = SSM_HEADS // SSM_GROUPS
GROUP_W = GROUP_HEADS * HEAD_DIM
SSM_STATE = 128
SSM_CONV = 7
CHUNK = 128
ATTN_HEADS = 16
KV_HEADS = 4
KV_REP = ATTN_HEADS // KV_HEADS
WINDOW = 128
QBLK = 128
KEY_SPAN = QBLK + 2 * WINDOW
KEY_TILES = KEY_SPAN // QBLK
REL_BUCKETS = 32
REL_MAX_DIST = 128
D_FF = 2816
FFN_CONV = 3
FF_CHUNK = 256
EPS = 1e-6
LOG2E = math.log2(math.e)
LANES = 128
SUBLANES = 8
HALO_ROWS = 2 * SUBLANES
MXU_SHIFT_TAPS = (6,)

COL_Z, COL_Q, COL_XS, COL_B, COL_C, COL_K, COL_V = 0, 1024, 2048, 3072, 3328, 3584, 3840
PROJ_COLS = 4096

VMEM_LIMIT = 56 * 1024 * 1024


def _cparams(*sem):
    return pltpu.CompilerParams(dimension_semantics=sem, vmem_limit_bytes=VMEM_LIMIT)


def _sigmoid(x):
    return 1.0 / (1.0 + jnp.exp(-x))


def _softplus(x):
    return jnp.maximum(x, 0.0) + jnp.log1p(jnp.exp(-jnp.abs(x)))


def _rms(x, w):
    ms = jnp.mean(x * x, axis=-1, keepdims=True)
    return x * lax.rsqrt(ms + EPS) * w


def _split3(x):
    hi = x.astype(BF16)
    r1 = x - hi.astype(F32)
    mid = r1.astype(BF16)
    lo = (r1 - mid.astype(F32)).astype(BF16)
    return hi, mid, lo


def _dot(a, b):
    return jnp.dot(a, b, preferred_element_type=F32)


def _dot_nt(a, b):
    return lax.dot_general(a, b, (((1,), (1,)), ((), ())), preferred_element_type=F32)


def _inproj_kernel(x_ref, nw_ref, w_ref, wdt_ref, proj_ref, dt_ref, *, tn):
    hb = _rms(x_ref[...], nw_ref[...]).astype(BF16)
    for g in range(SSM_GROUPS):
        dt_ref[g] = _dot(hb, wdt_ref[g])
    for j in range(PROJ_COLS // tn):
        proj_ref[:, j * tn:(j + 1) * tn] = _dot(hb, w_ref[:, j * tn:(j + 1) * tn]).astype(BF16)


def _inproj(x2, nw, w, wdt, tm=1024, tn=1024):
    n = x2.shape[0]
    return pl.pallas_call(
        functools.partial(_inproj_kernel, tn=tn),
        grid=(n // tm,),
        in_specs=[
            pl.BlockSpec((tm, D_MODEL), lambda i: (i, 0)),
            pl.BlockSpec((1, D_MODEL), lambda i: (0, 0)),
            pl.BlockSpec((D_MODEL, PROJ_COLS), lambda i: (0, 0)),
            pl.BlockSpec((SSM_GROUPS, D_MODEL, LANES), lambda i: (0, 0, 0)),
        ],
        out_specs=[
            pl.BlockSpec((tm, PROJ_COLS), lambda i: (i, 0)),
            pl.BlockSpec((SSM_GROUPS, tm, LANES), lambda i: (0, i, 0)),
        ],
        out_shape=[
            jax.ShapeDtypeStruct((n, PROJ_COLS), BF16),
            jax.ShapeDtypeStruct((SSM_GROUPS, n, LANES), F32),
        ],
        compiler_params=_cparams("parallel"),
        name="inproj",
    )(x2, nw, w, wdt)


def _ssd_kernel(xs_ref, b_ref, c_ref, dt_ref, cwx_ref, cwb_ref, cwc_ref, cbx_ref, cbb_ref,
                cbc_ref, dtb_ref, alog_ref, dskip_ref, e_ref, o_ref,
                xs_s, bt_s, c_s, cs_s, ut_s, cb_s, dec_s, sf_s, sb_s, ew_r, mix_r, *, seq):
    nchunk = seq // CHUNK
    npair = GROUP_W // LANES
    lane = lax.broadcasted_iota(jnp.int32, (CHUNK, LANES), 1)
    row = lax.broadcasted_iota(jnp.int32, (CHUNK, LANES), 0)
    is_fwd_lane = lane < GROUP_HEADS
    lo_half = lane < HEAD_DIM
    lt = lane < row
    gt = lane > row
    row2 = lax.broadcasted_iota(jnp.int32, (2 * CHUNK, LANES), 0)
    lane2 = lax.broadcasted_iota(jnp.int32, (2 * CHUNK, LANES), 1)
    tri2 = jnp.where(row2 < CHUNK, jnp.where(lane2 <= row2, 1.0, 0.0),
                     jnp.where(lane2 >= row2 - CHUNK, 1.0, 0.0)).astype(BF16)

    half = SSM_CONV // 2
    win_rows = CHUNK + 2 * HALO_ROWS
    mxu_taps = MXU_SHIFT_TAPS
    if mxu_taps:
        srow = lax.broadcasted_iota(jnp.int32, (len(mxu_taps) * CHUNK, win_rows), 0)
        scol = lax.broadcasted_iota(jnp.int32, (len(mxu_taps) * CHUNK, win_rows), 1)
        sblk = jnp.right_shift(srow, CHUNK.bit_length() - 1)
        soff = sum(jnp.where(sblk == i, HALO_ROWS - half + k, 0) for i, k in enumerate(mxu_taps))
        shift_mat = jnp.where(scol == srow - sblk * CHUNK + soff, 1.0, 0.0).astype(BF16)

    def conv_silu(src, w_ref, bias_ref, c, r0):
        lo = pl.multiple_of(jnp.maximum(r0 - HALO_ROWS, 0), HALO_ROWS)
        hi = pl.multiple_of(jnp.minimum(r0 + CHUNK, seq - HALO_ROWS), HALO_ROWS)
        prev = jnp.where(c > 0, src[pl.ds(lo, HALO_ROWS), :].astype(F32), 0.0)
        nxt = jnp.where(c < nchunk - 1, src[pl.ds(hi, HALO_ROWS), :].astype(F32), 0.0)
        cur = src[pl.ds(r0, CHUNK), :]
        curf = cur.astype(F32)
        if mxu_taps:
            win = jnp.concatenate([prev.astype(BF16), cur, nxt.astype(BF16)], axis=0)
            shifted = _dot(shift_mat, win)
        winf = jnp.concatenate([prev, curf, nxt], axis=0)
        acc = bias_ref[...] + curf * w_ref[half:half + 1, :]
        for i, k in enumerate(mxu_taps):
            acc = acc + shifted[i * CHUNK:(i + 1) * CHUNK] * w_ref[k:k + 1, :]
        for k in range(SSM_CONV):
            if k != half and k not in mxu_taps:
                off = HALO_ROWS - half + k
                acc = acc + winf[off:off + CHUNK] * w_ref[k:k + 1, :]
        return acc * _sigmoid(acc)

    a2_row = -jnp.exp(alog_ref[...]) * LOG2E
    e_mat = e_ref[...]

    def expand(q, exact=False):
        hi = q.astype(BF16)
        out = _dot(hi, e_mat)
        if exact:
            out = out + _dot((q - hi.astype(F32)).astype(BF16), e_mat)
        return out

    xs_s[0:CHUNK, :] = jnp.zeros((CHUNK, GROUP_W), F32)
    bt_s[0] = jnp.zeros((SSM_STATE, CHUNK), BF16)
    c_s[0:CHUNK, :] = jnp.zeros((CHUNK, SSM_STATE), BF16)
    ew_r[...] = jnp.zeros_like(ew_r)

    def prep_finish(cf, slot):
        rf = pl.multiple_of(cf * CHUNK, CHUNK)
        ewf = ew_r[slot]
        xf = xs_s[pl.ds(rf, CHUNK), :]
        btf = bt_s[cf]
        sf_s[cf + 1] = _dot(btf, (xf * ewf[:, :GROUP_W]).astype(BF16))
        sb_s[cf] = _dot(btf, (xf * ewf[:, GROUP_W:]).astype(BF16))
        cb_s[cf] = _dot(c_s[pl.ds(rf, CHUNK), :], btf)

    def prep_body(c, carry):
        prep_finish(jnp.maximum(c - 1, 0), (c + 1) % 2)
        r0 = pl.multiple_of(c * CHUNK, CHUNK)
        xs_s[pl.ds(r0, CHUNK), :] = conv_silu(xs_ref, cwx_ref, cbx_ref, c, r0)
        bt_s[c] = conv_silu(b_ref, cwb_ref, cbb_ref, c, r0).T.astype(BF16)
        c_s[pl.ds(r0, CHUNK), :] = conv_silu(c_ref, cwc_ref, cbc_ref, c, r0).astype(BF16)

        dt = _softplus(dt_ref[pl.ds(r0, CHUNK), :] + dtb_ref[...])
        hi, mid, lo = _split3(dt * a2_row)
        cums = _dot(tri2, hi) + _dot(tri2, mid) + _dot(tri2, lo)
        cum, rcum = cums[:CHUNK], cums[CHUNK:]
        cs = jnp.where(is_fwd_lane, cum, rcum)
        tot = jnp.where(is_fwd_lane[:1], cum[CHUNK - 1:CHUNK], rcum[0:1])
        cs_s[pl.ds(r0, CHUNK), :] = cs
        dsum = dt + pltpu.roll(dt, LANES - GROUP_HEADS, axis=1)
        ut_s[c, 0:2 * GROUP_HEADS, :] = (jnp.log2(dt) - cs).T[0:2 * GROUP_HEADS]
        ut_s[c, 2 * GROUP_HEADS:3 * GROUP_HEADS, :] = jnp.log2(dsum).T[0:GROUP_HEADS]
        dec_s[c] = expand(jnp.exp2(jnp.broadcast_to(tot, (SUBLANES, LANES))), exact=True)
        ew_r[c % 2] = expand(jnp.exp2(tot - cs) * dt)
        return carry

    lax.fori_loop(0, nchunk, prep_body, 0)
    prep_finish(nchunk - 1, (nchunk - 1) % 2)

    zstate = jnp.zeros((SSM_STATE, GROUP_W), F32)
    sf_s[0] = zstate
    sb_s[nchunk] = zstate

    def fwd_scan(c, carry):
        sf_s[c] = sf_s[c - 1] * dec_s[c - 1][0:1, :GROUP_W] + sf_s[c]
        return carry

    def bwd_scan(i, carry):
        c = nchunk - 2 - i
        sb_s[c + 1] = sb_s[c + 2] * dec_s[c + 1][0:1, GROUP_W:] + sb_s[c + 1]
        return carry

    lax.fori_loop(1, nchunk, fwd_scan, 0)
    lax.fori_loop(0, nchunk - 1, bwd_scan, 0)

    dskip = dskip_ref[...]
    mix_r[...] = jnp.zeros_like(mix_r)

    def out_finish(cf, slot):
        rf = pl.multiple_of(cf * CHUNK, CHUNK)
        x32 = xs_s[pl.ds(rf, CHUNK), :]
        cc = c_s[pl.ds(rf, CHUNK), :]
        ys = []
        for j in range(npair):
            xp = x32[:, j * LANES:(j + 1) * LANES]
            ys.append(_dot(mix_r[slot, 2 * j], jnp.where(lo_half, xp, 0.0).astype(BF16))
                      + _dot(mix_r[slot, 2 * j + 1], jnp.where(lo_half, 0.0, xp).astype(BF16)))
        y = jnp.concatenate(ys, axis=1)
        ee = expand(jnp.exp2(cs_s[pl.ds(rf, CHUNK), :]))
        y = (y + _dot(cc, sf_s[cf].astype(BF16)) * ee[:, :GROUP_W]
             + _dot(cc, sb_s[cf + 1].astype(BF16)) * ee[:, GROUP_W:])
        o_ref[pl.ds(rf, CHUNK), :] = (y + dskip * x32).astype(BF16)

    def out_body(c, carry):
        out_finish(jnp.maximum(c - 1, 0), (c + 1) % 2)
        r0 = pl.multiple_of(c * CHUNK, CHUNK)
        cs = cs_s[pl.ds(r0, CHUNK), :]
        cb = cb_s[c]
        for r in range(GROUP_HEADS):
            rb = GROUP_HEADS + r
            arg_f = cs[:, r:r + 1] + ut_s[c, r:r + 1, :]
            arg_b = cs[:, rb:rb + 1] + ut_s[c, rb:rb + 1, :]
            arg = jnp.where(lt, arg_f, jnp.where(gt, arg_b, ut_s[c, rb + GROUP_HEADS:rb + GROUP_HEADS + 1, :]))
            mix_r[c % 2, r] = (cb * jnp.exp2(arg)).astype(BF16)
        return carry

    lax.fori_loop(0, nchunk, out_body, 0)
    out_finish(nchunk - 1, (nchunk - 1) % 2)


def _ssd(proj, dt, cw, cb, dtb, alog, dskip, e_mat, batch, seq):
    n = batch * seq
    nchunk = seq // CHUNK
    wb = GROUP_W // LANES
    return pl.pallas_call(
        functools.partial(_ssd_kernel, seq=seq),
        grid=(batch, SSM_GROUPS),
        in_specs=[
            pl.BlockSpec((seq, GROUP_W), lambda b, g: (b, COL_XS // GROUP_W + g)),
            pl.BlockSpec((seq, LANES), lambda b, g: (b, COL_B // LANES + g)),
            pl.BlockSpec((seq, LANES), lambda b, g: (b, COL_C // LANES + g)),
            pl.BlockSpec((None, seq, LANES), lambda b, g: (g, b, 0)),
            pl.BlockSpec((SSM_CONV, GROUP_W), lambda b, g: (0, g)),
            pl.BlockSpec((SSM_CONV, LANES), lambda b, g: (0, SSM_GROUPS * wb + g)),
            pl.BlockSpec((SSM_CONV, LANES), lambda b, g: (0, SSM_GROUPS * wb + SSM_GROUPS + g)),
            pl.BlockSpec((1, GROUP_W), lambda b, g: (0, g)),
            pl.BlockSpec((1, LANES), lambda b, g: (0, SSM_GROUPS * wb + g)),
            pl.BlockSpec((1, LANES), lambda b, g: (0, SSM_GROUPS * wb + SSM_GROUPS + g)),
            pl.BlockSpec((None, 1, LANES), lambda b, g: (g, 0, 0)),
            pl.BlockSpec((None, 1, LANES), lambda b, g: (g, 0, 0)),
            pl.BlockSpec((1, GROUP_W), lambda b, g: (0, g)),
            pl.BlockSpec((LANES, 2 * GROUP_W), lambda b, g: (0, 0)),
        ],
        out_specs=pl.BlockSpec((seq, GROUP_W), lambda b, g: (b, g)),
        out_shape=jax.ShapeDtypeStruct((n, SSM_GROUPS * GROUP_W), BF16),
        scratch_shapes=[
            pltpu.VMEM((seq, GROUP_W), F32),
            pltpu.VMEM((nchunk, SSM_STATE, CHUNK), BF16),
            pltpu.VMEM((seq, SSM_STATE), BF16),
            pltpu.VMEM((seq, LANES), F32),
            pltpu.VMEM((nchunk, 3 * GROUP_HEADS, LANES), F32),
            pltpu.VMEM((nchunk, CHUNK, CHUNK), F32),
            pltpu.VMEM((nchunk, SUBLANES, 2 * GROUP_W), F32),
            pltpu.VMEM((nchunk + 1, SSM_STATE, GROUP_W), F32),
            pltpu.VMEM((nchunk + 1, SSM_STATE, GROUP_W), F32),
            pltpu.VMEM((2, CHUNK, 2 * GROUP_W), F32),
            pltpu.VMEM((2, GROUP_HEADS, CHUNK, CHUNK), BF16),
        ],
        compiler_params=_cparams("parallel", "parallel"),
        name="ssd",
    )(proj, proj, proj, dt, cw, cw, cw, cb, cb, cb, dtb, alog, dskip, e_mat)


def _attn_kernel(sink_ref, q_ref, k_ref, v_ref, bias_ref, o_ref, ke_s, ko_s, vt_s, *, seq):
    nblk = seq // QBLK
    kvw = KV_HEADS * HEAD_DIM
    ncol = kvw // LANES
    lane = lax.broadcasted_iota(jnp.int32, (QBLK, LANES), 1)
    lo_half = lane < HEAD_DIM

    zrow = jnp.zeros((WINDOW, KV_HEADS * LANES), BF16)
    for s in (ke_s, ko_s):
        s[0:WINDOW, :] = zrow
        s[WINDOW + seq:WINDOW + seq + WINDOW, :] = zrow
    zcol = jnp.zeros((kvw, WINDOW), BF16)
    vt_s[0] = zcol
    vt_s[nblk + 1] = zcol

    def fill_body(c, carry):
        r0 = pl.multiple_of(c * QBLK, QBLK)
        dst = pl.ds(WINDOW + r0, QBLK)
        kc = k_ref[pl.ds(r0, QBLK), :].astype(F32)
        for j in range(ncol):
            col = kc[:, j * LANES:(j + 1) * LANES]
            swp = pltpu.roll(col, HEAD_DIM, axis=1)
            g0, g1 = 2 * j, 2 * j + 1
            ke_s[dst, g0 * LANES:(g0 + 1) * LANES] = jnp.where(lo_half, col, 0.0).astype(BF16)
            ko_s[dst, g0 * LANES:(g0 + 1) * LANES] = jnp.where(lo_half, 0.0, swp).astype(BF16)
            ke_s[dst, g1 * LANES:(g1 + 1) * LANES] = jnp.where(lo_half, swp, 0.0).astype(BF16)
            ko_s[dst, g1 * LANES:(g1 + 1) * LANES] = jnp.where(lo_half, 0.0, col).astype(BF16)
        vt_s[c + 1] = v_ref[pl.ds(r0, QBLK), :].astype(F32).T.astype(BF16)
        return carry

    lax.fori_loop(0, nblk, fill_body, 0)

    def body(n, carry):
        r0 = pl.multiple_of(n * QBLK, QBLK)
        starts = (pl.multiple_of(jnp.where(n == 0, KEY_SPAN, 0), QBLK),
                  QBLK,
                  pl.multiple_of(jnp.where(n == nblk - 1, KEY_SPAN, KEY_SPAN - QBLK), QBLK))
        outs = []
        for pr in range(ATTN_HEADS // 2):
            qp = q_ref[pl.ds(r0, QBLK), pr * LANES:(pr + 1) * LANES]
            g = (2 * pr) // KV_REP
            vts = [vt_s[n + w, g * HEAD_DIM:(g + 1) * HEAD_DIM, :] for w in range(KEY_TILES)]
            for par, ks in ((0, ke_s), (1, ko_s)):
                h = 2 * pr + par
                sk = sink_ref[h]
                bias = jnp.concatenate([bias_ref[h, pl.ds(starts[w], QBLK), :] for w in range(KEY_TILES)], axis=0)
                t = _dot_nt(ks[pl.ds(r0, KEY_SPAN), g * LANES:(g + 1) * LANES], qp) + bias
                m = jnp.maximum(jnp.max(t, axis=0, keepdims=True), sk)
                p = jnp.exp(t - m)
                denom = jnp.sum(p, axis=0, keepdims=True) + jnp.exp(sk - m)
                pb = p.astype(BF16)
                pv = sum(_dot(vts[w], pb[w * QBLK:(w + 1) * QBLK]) for w in range(KEY_TILES))
                outs.append(pv * (1.0 / denom))
        o_ref[pl.ds(r0, QBLK), :] = jnp.concatenate(outs, axis=0).T.astype(BF16)
        return carry

    lax.fori_loop(0, nblk, body, 0)


def _attn(proj, bias_t, sink, batch, seq):
    n = batch * seq
    aw = ATTN_HEADS * HEAD_DIM
    kvw = KV_HEADS * HEAD_DIM
    return pl.pallas_call(
        functools.partial(_attn_kernel, seq=seq),
        grid=(batch,),
        in_specs=[
            pl.BlockSpec(memory_space=pltpu.SMEM),
            pl.BlockSpec((seq, aw), lambda b: (b, COL_Q // aw)),
            pl.BlockSpec((seq, kvw), lambda b: (b, COL_K // kvw)),
            pl.BlockSpec((seq, kvw), lambda b: (b, COL_V // kvw)),
            pl.BlockSpec((ATTN_HEADS, KEY_SPAN + QBLK, QBLK), lambda b: (0, 0, 0)),
        ],
        out_specs=pl.BlockSpec((seq, aw), lambda b: (b, 0)),
        out_shape=jax.ShapeDtypeStruct((n, aw), BF16),
        scratch_shapes=[
            pltpu.VMEM((seq + 2 * WINDOW, KV_HEADS * LANES), BF16),
            pltpu.VMEM((seq + 2 * WINDOW, KV_HEADS * LANES), BF16),
            pltpu.VMEM((seq // QBLK + 2, kvw, QBLK), BF16),
        ],
        compiler_params=_cparams("parallel"),
        name="attn",
    )(sink, proj, proj, proj, bias_t)


def _outproj_kernel(x_ref, ys_ref, z_ref, ya_ref, nw_ref, ws_ref, wa_ref, o_ref):
    z = z_ref[...].astype(F32)
    yg = ys_ref[...].astype(F32) * (z * _sigmoid(z))
    ysn = jnp.concatenate(
        [_rms(yg[:, g * GROUP_W:(g + 1) * GROUP_W], nw_ref[:, g * GROUP_W:(g + 1) * GROUP_W]).astype(BF16)
         for g in range(SSM_GROUPS)], axis=1)
    o_ref[...] = x_ref[...] + _dot(ysn, ws_ref[...]) + _dot(ya_ref[...], wa_ref[...])


def _outproj(x2, ys, proj, ya, nw, ws, wa, tm=1024):
    n = x2.shape[0]
    return pl.pallas_call(
        _outproj_kernel,
        grid=(n // tm,),
        in_specs=[
            pl.BlockSpec((tm, D_MODEL), lambda i: (i, 0)),
            pl.BlockSpec((tm, D_MODEL), lambda i: (i, 0)),
            pl.BlockSpec((tm, D_MODEL), lambda i: (i, COL_Z // D_MODEL)),
            pl.BlockSpec((tm, D_MODEL), lambda i: (i, 0)),
            pl.BlockSpec((1, D_MODEL), lambda i: (0, 0)),
            pl.BlockSpec((D_MODEL, D_MODEL), lambda i: (0, 0)),
            pl.BlockSpec((D_MODEL, D_MODEL), lambda i: (0, 0)),
        ],
        out_specs=pl.BlockSpec((tm, D_MODEL), lambda i: (i, 0)),
        out_shape=jax.ShapeDtypeStruct((n, D_MODEL), F32),
        compiler_params=_cparams("parallel"),
        name="outproj",
    )(x2, ys, proj, ya, nw, ws, wa)


def _ffn_kernel(x_ref, nw_ref, wg_ref, wu_ref, cw_ref, cb_ref, wd_ref, fw_ref, o_ref, h_ref, *, seq, final):
    j = pl.program_id(1)

    @pl.when(j == 0)
    def _():
        x = x_ref[...]
        h_ref[...] = _rms(x, nw_ref[...]).astype(BF16)
        o_ref[...] = x

    h = h_ref[...]
    g = _dot(h, wg_ref[...])
    u = _dot(h, wu_ref[...])
    cw = cw_ref[...]
    zpad = jnp.zeros((SUBLANES, FF_CHUNK), F32)
    gp = jnp.concatenate([zpad, g, zpad], axis=0)
    gc = cb_ref[...] + g * cw[1:2]
    gc = gc + gp[SUBLANES - 1:SUBLANES - 1 + seq] * cw[0:1] + gp[SUBLANES + 1:SUBLANES + 1 + seq] * cw[2:3]
    act = (gc * _sigmoid(gc) * u).astype(BF16)
    o_ref[...] += _dot(act, wd_ref[...])

    if final:
        @pl.when(j == pl.num_programs(1) - 1)
        def _():
            o_ref[...] = _rms(o_ref[...], fw_ref[...])


def _ffn(x2, nw, wup, cw, cb, wd, fw, batch, seq, final):
    n = batch * seq
    nck = D_FF // FF_CHUNK
    return pl.pallas_call(
        functools.partial(_ffn_kernel, seq=seq, final=final),
        grid=(batch, nck),
        in_specs=[
            pl.BlockSpec((seq, D_MODEL), lambda b, j: (b, 0)),
            pl.BlockSpec((1, D_MODEL), lambda b, j: (0, 0)),
            pl.BlockSpec((None, D_MODEL, FF_CHUNK), lambda b, j: (j, 0, 0)),
            pl.BlockSpec((None, D_MODEL, FF_CHUNK), lambda b, j: (nck + j, 0, 0)),
            pl.BlockSpec((FFN_CONV, FF_CHUNK), lambda b, j: (0, j)),
            pl.BlockSpec((1, FF_CHUNK), lambda b, j: (0, j)),
            pl.BlockSpec((FF_CHUNK, D_MODEL), lambda b, j: (j, 0)),
            pl.BlockSpec((1, D_MODEL), lambda b, j: (0, 0)),
        ],
        out_specs=pl.BlockSpec((seq, D_MODEL), lambda b, j: (b, 0)),
        out_shape=jax.ShapeDtypeStruct((n, D_MODEL), F32),
        scratch_shapes=[pltpu.VMEM((seq, D_MODEL), BF16)],
        compiler_params=_cparams("parallel", "arbitrary"),
        name="ffn_final" if final else "ffn",
    )(x2, nw, wup, wup, cw, cb, wd, fw)


def _t5_bucket(rel):
    half = REL_BUCKETS // 2
    max_exact = half // 2
    ret = jnp.where(rel > 0, half, 0)
    n = jnp.abs(rel)
    nf = jnp.maximum(n, 1).astype(F32)
    large = max_exact + (jnp.log(nf / max_exact) / math.log(REL_MAX_DIST / max_exact)
                         * (half - max_exact)).astype(jnp.int32)
    large = jnp.minimum(large, half - 1)
    return ret + jnp.where(n < max_exact, n, large)


def _band_bias_t(rel_bias):
    rel = jnp.arange(KEY_SPAN)[:, None] - WINDOW - jnp.arange(QBLK)[None, :]
    onehot = (_t5_bucket(rel)[..., None] == jnp.arange(REL_BUCKETS)).astype(F32)
    bias = jnp.einsum("jib,bh->hji", onehot, rel_bias.astype(F32), precision=lax.Precision.HIGHEST)
    bias = jnp.where((jnp.abs(rel) <= WINDOW)[None], bias, -jnp.inf)
    return jnp.concatenate([bias, jnp.full((ATTN_HEADS, QBLK, QBLK), -jnp.inf, F32)], axis=1)


def _head_lanes(p):
    q = p.astype(F32).reshape(2, SSM_GROUPS, GROUP_HEADS).transpose(1, 0, 2).reshape(SSM_GROUPS, 2 * GROUP_HEADS)
    return jnp.pad(q, ((0, 0), (0, LANES - 2 * GROUP_HEADS)))[:, None, :]


def _expand_matrix():
    j = jnp.arange(LANES)[:, None]
    col = jnp.arange(2 * GROUP_W)[None, :]
    return jnp.where((j < 2 * GROUP_HEADS) & (col // HEAD_DIM == j), 1.0, 0.0).astype(BF16)


def kernel(x, rel_bias, norm1_w, w_in, conv_w, conv_b, dt_bias, a_log, d_skip, ssm_norm_w, attn_sink,
           w_out, norm2_w, w_up, ffn_conv_w, ffn_conv_b, w_down, final_norm_w):
    batch, seq, _ = x.shape
    assert seq % CHUNK == 0 and seq % QBLK == 0
    n = batch * seq
    depth = w_in.shape[0]
    x2 = x.reshape(n, D_MODEL)
    bias_t = _band_bias_t(rel_bias)
    e_mat = _expand_matrix()
    zw, xw_end = 1024, 2560
    dt_end = xw_end + 2 * SSM_HEADS
    q_end = dt_end + ATTN_HEADS * HEAD_DIM
    scale = HEAD_DIM ** -0.5
    nck = D_FF // FF_CHUNK

    for i in range(depth):
        wi = w_in[i]
        w_main = jnp.concatenate(
            [wi[:, :zw], wi[:, dt_end:q_end] * scale, wi[:, zw:xw_end], wi[:, q_end:]], axis=1).astype(BF16)
        wdt = wi[:, xw_end:dt_end].reshape(D_MODEL, 2, SSM_GROUPS, GROUP_HEADS).transpose(2, 0, 1, 3)
        wdt = jnp.pad(wdt.reshape(SSM_GROUPS, D_MODEL, 2 * GROUP_HEADS),
                      ((0, 0), (0, 0), (0, LANES - 2 * GROUP_HEADS))).astype(BF16)
        proj, dt = _inproj(x2, norm1_w[i][None], w_main, wdt)
        y_ssm = _ssd(proj, dt, conv_w[i], conv_b[i][None], _head_lanes(dt_bias[i]), _head_lanes(a_log[i]),
                     jnp.repeat(d_skip[i].astype(F32), HEAD_DIM)[None], e_mat, batch, seq)
        y_attn = _attn(proj, bias_t, attn_sink[i].astype(F32), batch, seq)
        wo = w_out[i].astype(BF16)
        x2 = _outproj(x2, y_ssm, proj, y_attn, ssm_norm_w[i][None], wo[:D_MODEL], wo[D_MODEL:])
        wup = w_up[i].astype(BF16).reshape(D_MODEL, 2 * nck, FF_CHUNK).transpose(1, 0, 2)
        x2 = _ffn(x2, norm2_w[i][None], wup, ffn_conv_w[i], ffn_conv_b[i][None],
                  w_down[i].astype(BF16), final_norm_w[None], batch, seq, final=(i == depth - 1))

    return x2.reshape(batch, seq, D_MODEL)
```

```python
import functools
import math

import jax
import jax.numpy as jnp
from jax import lax
from jax.experimental import pallas as pl
from jax.experimental.pallas import tpu as pltpu

F32 = jnp.float32
BF16 = jnp.bfloat16

D_MODEL = 1024
HEAD_DIM = 64
SSM_HEADS = 16
SSM_GROUPS = 2
GROUP_HEADS = SSM_HEADS // SSM_GROUPS
GROUP_W = GROUP_HEADS * HEAD_DIM
SSM_STATE = 128
SSM_CONV = 7
CHUNK = 128
ATTN_HEADS = 16
KV_HEADS = 4
KV_REP = ATTN_HEADS // KV_HEADS
WINDOW = 128
QBLK = 128
KEY_SPAN = QBLK + 2 * WINDOW
KEY_TILES = KEY_SPAN // QBLK
REL_BUCKETS = 32
REL_MAX_DIST = 128
D_FF = 2816
FFN_CONV = 3
FF_CHUNK = 256
FF_ROWS = 1024
EPS = 1e-6
LOG2E = math.log2(math.e)
LANES = 128
SUBLANES = 8
HALO_ROWS = 2 * SUBLANES
MXU_SHIFT_TAPS = (6,)

COL_Z, COL_Q, COL_XS, COL_B, COL_C, COL_K, COL_V = 0, 1024, 2048, 3072, 3328, 3584, 3840
PROJ_COLS = 4096

VMEM_LIMIT = 56 * 1024 * 1024


def _cparams(*sem):
    return pltpu.CompilerParams(dimension_semantics=sem, vmem_limit_bytes=VMEM_LIMIT)


def _sigmoid(x):
    return 1.0 / (1.0 + jnp.exp(-x))


def _softplus(x):
    return jnp.maximum(x, 0.0) + jnp.log1p(jnp.exp(-jnp.abs(x)))


def _rms(x, w):
    ms = jnp.mean(x * x, axis=-1, keepdims=True)
    return x * lax.rsqrt(ms + EPS) * w


def _split3(x):
    hi = x.astype(BF16)
    r1 = x - hi.astype(F32)
    mid = r1.astype(BF16)
    lo = (r1 - mid.astype(F32)).astype(BF16)
    return hi, mid, lo


def _dot(a, b):
    return jnp.dot(a, b, preferred_element_type=F32)


def _dot_nt(a, b):
    return lax.dot_general(a, b, (((1,), (1,)), ((), ())), preferred_element_type=F32)


def _inproj_kernel(x_ref, nw_ref, w_ref, wdt_ref, proj_ref, dt_ref, *, tn):
    hb = _rms(x_ref[...], nw_ref[...]).astype(BF16)
    for g in range(SSM_GROUPS):
        dt_ref[g] = _dot(hb, wdt_ref[g])
    for j in range(PROJ_COLS // tn):
        proj_ref[:, j * tn:(j + 1) * tn] = _dot(hb, w_ref[:, j * tn:(j + 1) * tn]).astype(BF16)


def _inproj(x2, nw, w, wdt, tm=1024, tn=1024):
    n = x2.shape[0]
    return pl.pallas_call(
        functools.partial(_inproj_kernel, tn=tn),
        grid=(n // tm,),
        in_specs=[
            pl.BlockSpec((tm, D_MODEL), lambda i: (i, 0)),
            pl.BlockSpec((1, D_MODEL), lambda i: (0, 0)),
            pl.BlockSpec((D_MODEL, PROJ_COLS), lambda i: (0, 0)),
            pl.BlockSpec((SSM_GROUPS, D_MODEL, LANES), lambda i: (0, 0, 0)),
        ],
        out_specs=[
            pl.BlockSpec((tm, PROJ_COLS), lambda i: (i, 0)),
            pl.BlockSpec((SSM_GROUPS, tm, LANES), lambda i: (0, i, 0)),
        ],
        out_shape=[
            jax.ShapeDtypeStruct((n, PROJ_COLS), BF16),
            jax.ShapeDtypeStruct((SSM_GROUPS, n, LANES), F32),
        ],
        compiler_params=_cparams("parallel"),
        name="inproj",
    )(x2, nw, w, wdt)


def _ssd_kernel(xs_ref, b_ref, c_ref, dt_ref, cwx_ref, cwb_ref, cwc_ref, cbx_ref, cbb_ref,
                cbc_ref, dtb_ref, alog_ref, dskip_ref, e_ref, o_ref,
                xs_s, bt_s, c_s, cs_s, ut_s, cb_s, dec_s, sf_s, sb_s, ew_r, mix_r, *, seq):
    nchunk = seq // CHUNK
    npair = GROUP_W // LANES
    lane = lax.broadcasted_iota(jnp.int32, (CHUNK, LANES), 1)
    row = lax.broadcasted_iota(jnp.int32, (CHUNK, LANES), 0)
    is_fwd_lane = lane < GROUP_HEADS
    lo_half = lane < HEAD_DIM
    lt = lane < row
    gt = lane > row
    row2 = lax.broadcasted_iota(jnp.int32, (2 * CHUNK, LANES), 0)
    lane2 = lax.broadcasted_iota(jnp.int32, (2 * CHUNK, LANES), 1)
    tri2 = jnp.where(row2 < CHUNK, jnp.where(lane2 <= row2, 1.0, 0.0),
                     jnp.where(lane2 >= row2 - CHUNK, 1.0, 0.0)).astype(BF16)

    half = SSM_CONV // 2
    win_rows = CHUNK + 2 * HALO_ROWS
    mxu_taps = MXU_SHIFT_TAPS
    if mxu_taps:
        srow = lax.broadcasted_iota(jnp.int32, (len(mxu_taps) * CHUNK, win_rows), 0)
        scol = lax.broadcasted_iota(jnp.int32, (len(mxu_taps) * CHUNK, win_rows), 1)
        sblk = jnp.right_shift(srow, CHUNK.bit_length() - 1)
        soff = sum(jnp.where(sblk == i, HALO_ROWS - half + k, 0) for i, k in enumerate(mxu_taps))
        shift_mat = jnp.where(scol == srow - sblk * CHUNK + soff, 1.0, 0.0).astype(BF16)

    def conv_silu(src, w_ref, bias_ref, c, r0):
        lo = pl.multiple_of(jnp.maximum(r0 - HALO_ROWS, 0), HALO_ROWS)
        hi = pl.multiple_of(jnp.minimum(r0 + CHUNK, seq - HALO_ROWS), HALO_ROWS)
        prev = jnp.where(c > 0, src[pl.ds(lo, HALO_ROWS), :].astype(F32), 0.0)
        nxt = jnp.where(c < nchunk - 1, src[pl.ds(hi, HALO_ROWS), :].astype(F32), 0.0)
        cur = src[pl.ds(r0, CHUNK), :]
        curf = cur.astype(F32)
        if mxu_taps:
            win = jnp.concatenate([prev.astype(BF16), cur, nxt.astype(BF16)], axis=0)
            shifted = _dot(shift_mat, win)
        winf = jnp.concatenate([prev, curf, nxt], axis=0)
        acc = bias_ref[...] + curf * w_ref[half:half + 1, :]
        for i, k in enumerate(mxu_taps):
            acc = acc + shifted[i * CHUNK:(i + 1) * CHUNK] * w_ref[k:k + 1, :]
        for k in range(SSM_CONV):
            if k != half and k not in mxu_taps:
                off = HALO_ROWS - half + k
                acc = acc + winf[off:off + CHUNK] * w_ref[k:k + 1, :]
        return acc * _sigmoid(acc)

    a2_row = -jnp.exp(alog_ref[...]) * LOG2E
    e_mat = e_ref[...]

    def expand(q, exact=False):
        hi = q.astype(BF16)
        out = _dot(hi, e_mat)
        if exact:
            out = out + _dot((q - hi.astype(F32)).astype(BF16), e_mat)
        return out

    xs_s[0:CHUNK, :] = jnp.zeros((CHUNK, GROUP_W), F32)
    bt_s[0] = jnp.zeros((SSM_STATE, CHUNK), BF16)
    c_s[0:CHUNK, :] = jnp.zeros((CHUNK, SSM_STATE), BF16)
    ew_r[...] = jnp.zeros_like(ew_r)

    def prep_finish(cf, slot):
        rf = pl.multiple_of(cf * CHUNK, CHUNK)
        ewf = ew_r[slot]
        xf = xs_s[pl.ds(rf, CHUNK), :]
        btf = bt_s[cf]
        sf_s[cf + 1] = _dot(btf, (xf * ewf[:, :GROUP_W]).astype(BF16))
        sb_s[cf] = _dot(btf, (xf * ewf[:, GROUP_W:]).astype(BF16))
        cb_s[cf] = _dot(c_s[pl.ds(rf, CHUNK), :], btf)

    def prep_body(c, carry):
        prep_finish(jnp.maximum(c - 1, 0), (c + 1) % 2)
        r0 = pl.multiple_of(c * CHUNK, CHUNK)
        xs_s[pl.ds(r0, CHUNK), :] = conv_silu(xs_ref, cwx_ref, cbx_ref, c, r0)
        bt_s[c] = conv_silu(b_ref, cwb_ref, cbb_ref, c, r0).T.astype(BF16)
        c_s[pl.ds(r0, CHUNK), :] = conv_silu(c_ref, cwc_ref, cbc_ref, c, r0).astype(BF16)

        dt = _softplus(dt_ref[pl.ds(r0, CHUNK), :] + dtb_ref[...])
        hi, mid, lo = _split3(dt * a2_row)
        cums = _dot(tri2, hi) + _dot(tri2, mid) + _dot(tri2, lo)
        cum, rcum = cums[:CHUNK], cums[CHUNK:]
        cs = jnp.where(is_fwd_lane, cum, rcum)
        tot = jnp.where(is_fwd_lane[:1], cum[CHUNK - 1:CHUNK], rcum[0:1])
        cs_s[pl.ds(r0, CHUNK), :] = cs
        dsum = dt + pltpu.roll(dt, LANES - GROUP_HEADS, axis=1)
        ut_s[c, 0:2 * GROUP_HEADS, :] = (jnp.log2(dt) - cs).T[0:2 * GROUP_HEADS]
        ut_s[c, 2 * GROUP_HEADS:3 * GROUP_HEADS, :] = jnp.log2(dsum).T[0:GROUP_HEADS]
        dec_s[c] = expand(jnp.exp2(jnp.broadcast_to(tot, (SUBLANES, LANES))), exact=True)
        ew_r[c % 2] = expand(jnp.exp2(tot - cs) * dt)
        return carry

    lax.fori_loop(0, nchunk, prep_body, 0)
    prep_finish(nchunk - 1, (nchunk - 1) % 2)

    zstate = jnp.zeros((SSM_STATE, GROUP_W), F32)
    sf_s[0] = zstate
    sb_s[nchunk] = zstate

    def fwd_scan(c, carry):
        sf_s[c] = sf_s[c - 1] * dec_s[c - 1][0:1, :GROUP_W] + sf_s[c]
        return carry

    def bwd_scan(i, carry):
        c = nchunk - 2 - i
        sb_s[c + 1] = sb_s[c + 2] * dec_s[c + 1][0:1, GROUP_W:] + sb_s[c + 1]
        return carry

    lax.fori_loop(1, nchunk, fwd_scan, 0)
    lax.fori_loop(0, nchunk - 1, bwd_scan, 0)

    dskip = dskip_ref[...]
    mix_r[...] = jnp.zeros_like(mix_r)

    def out_finish(cf, slot):
        rf = pl.multiple_of(cf * CHUNK, CHUNK)
        x32 = xs_s[pl.ds(rf, CHUNK), :]
        cc = c_s[pl.ds(rf, CHUNK), :]
        ys = []
        for j in range(npair):
            xp = x32[:, j * LANES:(j + 1) * LANES]
            ys.append(_dot(mix_r[slot, 2 * j], jnp.where(lo_half, xp, 0.0).astype(BF16))
                      + _dot(mix_r[slot, 2 * j + 1], jnp.where(lo_half, 0.0, xp).astype(BF16)))
        y = jnp.concatenate(ys, axis=1)
        ee = expand(jnp.exp2(cs_s[pl.ds(rf, CHUNK), :]))
        y = (y + _dot(cc, sf_s[cf].astype(BF16)) * ee[:, :GROUP_W]
             + _dot(cc, sb_s[cf + 1].astype(BF16)) * ee[:, GROUP_W:])
        o_ref[pl.ds(rf, CHUNK), :] = (y + dskip * x32).astype(BF16)

    def out_body(c, carry):
        out_finish(jnp.maximum(c - 1, 0), (c + 1) % 2)
        r0 = pl.multiple_of(c * CHUNK, CHUNK)
        cs = cs_s[pl.ds(r0, CHUNK), :]
        cb = cb_s[c]
        for r in range(GROUP_HEADS):
            rb = GROUP_HEADS + r
            arg_f = cs[:, r:r + 1] + ut_s[c, r:r + 1, :]
            arg_b = cs[:, rb:rb + 1] + ut_s[c, rb:rb + 1, :]
            arg = jnp.where(lt, arg_f, jnp.where(gt, arg_b, ut_s[c, rb + GROUP_HEADS:rb + GROUP_HEADS + 1, :]))
            mix_r[c % 2, r] = (cb * jnp.exp2(arg)).astype(BF16)
        return carry

    lax.fori_loop(0, nchunk, out_body, 0)
    out_finish(nchunk - 1, (nchunk - 1) % 2)


def _ssd(proj, dt, cw, cb, dtb, alog, dskip, e_mat, batch, seq):
    n = batch * seq
    nchunk = seq // CHUNK
    wb = GROUP_W // LANES
    return pl.pallas_call(
        functools.partial(_ssd_kernel, seq=seq),
        grid=(batch, SSM_GROUPS),
        in_specs=[
            pl.BlockSpec((seq, GROUP_W), lambda b, g: (b, COL_XS // GROUP_W + g)),
            pl.BlockSpec((seq, LANES), lambda b, g: (b, COL_B // LANES + g)),
            pl.BlockSpec((seq, LANES), lambda b, g: (b, COL_C // LANES + g)),
            pl.BlockSpec((None, seq, LANES), lambda b, g: (g, b, 0)),
            pl.BlockSpec((SSM_CONV, GROUP_W), lambda b, g: (0, g)),
            pl.BlockSpec((SSM_CONV, LANES), lambda b, g: (0, SSM_GROUPS * wb + g)),
            pl.BlockSpec((SSM_CONV, LANES), lambda b, g: (0, SSM_GROUPS * wb + SSM_GROUPS + g)),
            pl.BlockSpec((1, GROUP_W), lambda b, g: (0, g)),
            pl.BlockSpec((1, LANES), lambda b, g: (0, SSM_GROUPS * wb + g)),
            pl.BlockSpec((1, LANES), lambda b, g: (0, SSM_GROUPS * wb + SSM_GROUPS + g)),
            pl.BlockSpec((None, 1, LANES), lambda b, g: (g, 0, 0)),
            pl.BlockSpec((None, 1, LANES), lambda b, g: (g, 0, 0)),
            pl.BlockSpec((1, GROUP_W), lambda b, g: (0, g)),
            pl.BlockSpec((LANES, 2 * GROUP_W), lambda b, g: (0, 0)),
        ],
        out_specs=pl.BlockSpec((seq, GROUP_W), lambda b, g: (b, g)),
        out_shape=jax.ShapeDtypeStruct((n, SSM_GROUPS * GROUP_W), BF16),
        scratch_shapes=[
            pltpu.VMEM((seq, GROUP_W), F32),
            pltpu.VMEM((nchunk, SSM_STATE, CHUNK), BF16),
            pltpu.VMEM((seq, SSM_STATE), BF16),
            pltpu.VMEM((seq, LANES), F32),
            pltpu.VMEM((nchunk, 3 * GROUP_HEADS, LANES), F32),
            pltpu.VMEM((nchunk, CHUNK, CHUNK), F32),
            pltpu.VMEM((nchunk, SUBLANES, 2 * GROUP_W), F32),
            pltpu.VMEM((nchunk + 1, SSM_STATE, GROUP_W), F32),
            pltpu.VMEM((nchunk + 1, SSM_STATE, GROUP_W), F32),
            pltpu.VMEM((2, CHUNK, 2 * GROUP_W), F32),
            pltpu.VMEM((2, GROUP_HEADS, CHUNK, CHUNK), BF16),
        ],
        compiler_params=_cparams("parallel", "parallel"),
        name="ssd",
    )(proj, proj, proj, dt, cw, cw, cw, cb, cb, cb, dtb, alog, dskip, e_mat)


def _attn_kernel(sink_ref, q_ref, k_ref, v_ref, bias_ref, o_ref, ke_s, ko_s, vt_s, *, seq):
    nblk = seq // QBLK
    kvw = KV_HEADS * HEAD_DIM
    ncol = kvw // LANES
    lane = lax.broadcasted_iota(jnp.int32, (QBLK, LANES), 1)
    lo_half = lane < HEAD_DIM

    zrow = jnp.zeros((WINDOW, KV_HEADS * LANES), BF16)
    for s in (ke_s, ko_s):
        s[0:WINDOW, :] = zrow
        s[WINDOW + seq:WINDOW + seq + WINDOW, :] = zrow
    zcol = jnp.zeros((kvw, WINDOW), BF16)
    vt_s[0] = zcol
    vt_s[nblk + 1] = zcol

    def fill_body(c, carry):
        r0 = pl.multiple_of(c * QBLK, QBLK)
        dst = pl.ds(WINDOW + r0, QBLK)
        kc = k_ref[pl.ds(r0, QBLK), :].astype(F32)
        for j in range(ncol):
            col = kc[:, j * LANES:(j + 1) * LANES]
            swp = pltpu.roll(col, HEAD_DIM, axis=1)
            g0, g1 = 2 * j, 2 * j + 1
            ke_s[dst, g0 * LANES:(g0 + 1) * LANES] = jnp.where(lo_half, col, 0.0).astype(BF16)
            ko_s[dst, g0 * LANES:(g0 + 1) * LANES] = jnp.where(lo_half, 0.0, swp).astype(BF16)
            ke_s[dst, g1 * LANES:(g1 + 1) * LANES] = jnp.where(lo_half, swp, 0.0).astype(BF16)
            ko_s[dst, g1 * LANES:(g1 + 1) * LANES] = jnp.where(lo_half, 0.0, col).astype(BF16)
        vt_s[c + 1] = v_ref[pl.ds(r0, QBLK), :].astype(F32).T.astype(BF16)
        return carry

    lax.fori_loop(0, nblk, fill_body, 0)

    def body(n, carry):
        r0 = pl.multiple_of(n * QBLK, QBLK)
        starts = (pl.multiple_of(jnp.where(n == 0, KEY_SPAN, 0), QBLK),
                  QBLK,
                  pl.multiple_of(jnp.where(n == nblk - 1, KEY_SPAN, KEY_SPAN - QBLK), QBLK))
        outs = []
        for pr in range(ATTN_HEADS // 2):
            qp = q_ref[pl.ds(r0, QBLK), pr * LANES:(pr + 1) * LANES]
            g = (2 * pr) // KV_REP
            vts = [vt_s[n + w, g * HEAD_DIM:(g + 1) * HEAD_DIM, :] for w in range(KEY_TILES)]
            for par, ks in ((0, ke_s), (1, ko_s)):
                h = 2 * pr + par
                sk = sink_ref[h]
                bias = jnp.concatenate([bias_ref[h, pl.ds(starts[w], QBLK), :] for w in range(KEY_TILES)], axis=0)
                t = _dot_nt(ks[pl.ds(r0, KEY_SPAN), g * LANES:(g + 1) * LANES], qp) + bias
                m = jnp.maximum(jnp.max(t, axis=0, keepdims=True), sk)
                p = jnp.exp(t - m)
                denom = jnp.sum(p, axis=0, keepdims=True) + jnp.exp(sk - m)
                pb = p.astype(BF16)
                pv = sum(_dot(vts[w], pb[w * QBLK:(w + 1) * QBLK]) for w in range(KEY_TILES))
                outs.append(pv * (1.0 / denom))
        o_ref[pl.ds(r0, QBLK), :] = jnp.concatenate(outs, axis=0).T.astype(BF16)
        return carry

    lax.fori_loop(0, nblk, body, 0)


def _attn(proj, bias_t, sink, batch, seq):
    n = batch * seq
    aw = ATTN_HEADS * HEAD_DIM
    kvw = KV_HEADS * HEAD_DIM
    return pl.pallas_call(
        functools.partial(_attn_kernel, seq=seq),
        grid=(batch,),
        in_specs=[
            pl.BlockSpec(memory_space=pltpu.SMEM),
            pl.BlockSpec((seq, aw), lambda b: (b, COL_Q // aw)),
            pl.BlockSpec((seq, kvw), lambda b: (b, COL_K // kvw)),
            pl.BlockSpec((seq, kvw), lambda b: (b, COL_V // kvw)),
            pl.BlockSpec((ATTN_HEADS, KEY_SPAN + QBLK, QBLK), lambda b: (0, 0, 0)),
        ],
        out_specs=pl.BlockSpec((seq, aw), lambda b: (b, 0)),
        out_shape=jax.ShapeDtypeStruct((n, aw), BF16),
        scratch_shapes=[
            pltpu.VMEM((seq + 2 * WINDOW, KV_HEADS * LANES), BF16),
            pltpu.VMEM((seq + 2 * WINDOW, KV_HEADS * LANES), BF16),
            pltpu.VMEM((seq // QBLK + 2, kvw, QBLK), BF16),
        ],
        compiler_params=_cparams("parallel"),
        name="attn",
    )(sink, proj, proj, proj, bias_t)


def _outproj_kernel(x_ref, ys_ref, z_ref, ya_ref, nw_ref, ws_ref, wa_ref, o_ref):
    z = z_ref[...].astype(F32)
    yg = ys_ref[...].astype(F32) * (z * _sigmoid(z))
    ysn = jnp.concatenate(
        [_rms(yg[:, g * GROUP_W:(g + 1) * GROUP_W], nw_ref[:, g * GROUP_W:(g + 1) * GROUP_W]).astype(BF16)
         for g in range(SSM_GROUPS)], axis=1)
    o_ref[...] = x_ref[...] + _dot(ysn, ws_ref[...]) + _dot(ya_ref[...], wa_ref[...])


def _outproj(x2, ys, proj, ya, nw, ws, wa, tm=1024):
    n = x2.shape[0]
    return pl.pallas_call(
        _outproj_kernel,
        grid=(n // tm,),
        in_specs=[
            pl.BlockSpec((tm, D_MODEL), lambda i: (i, 0)),
            pl.BlockSpec((tm, D_MODEL), lambda i: (i, 0)),
            pl.BlockSpec((tm, D_MODEL), lambda i: (i, COL_Z // D_MODEL)),
            pl.BlockSpec((tm, D_MODEL), lambda i: (i, 0)),
            pl.BlockSpec((1, D_MODEL), lambda i: (0, 0)),
            pl.BlockSpec((D_MODEL, D_MODEL), lambda i: (0, 0)),
            pl.BlockSpec((D_MODEL, D_MODEL), lambda i: (0, 0)),
        ],
        out_specs=pl.BlockSpec((tm, D_MODEL), lambda i: (i, 0)),
        out_shape=jax.ShapeDtypeStruct((n, D_MODEL), F32),
        compiler_params=_cparams("parallel"),
        name="outproj",
    )(x2, ys, proj, ya, nw, ws, wa)


def _ffn_kernel(x_ref, xp_ref, xn_ref, nw_ref, wup_ref, cw_ref, cb_ref, wd_ref, fw_ref, o_ref, h_s, act_a, act_b,
                *, rows, tiles_per_seq, final):
    nck = D_FF // FF_CHUNK
    tile = pl.program_id(0) % tiles_per_seq
    x = x_ref[...]
    nw = nw_ref[...]
    h_s[...] = _rms(x, nw).astype(BF16)
    o_ref[...] = x
    halo = _rms(jnp.concatenate([xp_ref[...], xn_ref[...]], axis=0), nw)
    hrow = lax.broadcasted_iota(jnp.int32, halo.shape, 0)
    inside = jnp.where(hrow < SUBLANES, jnp.where(tile > 0, 1, 0), jnp.where(tile < tiles_per_seq - 1, 1, 0))
    halo = jnp.where(inside > 0, halo, 0.0).astype(BF16)

    def start(j, act_ref):
        wg = wup_ref[j]
        h = h_s[...]
        g = _dot(h, wg)
        u = _dot(h, wup_ref[nck + j])
        gh = _dot(halo, wg)
        gp = jnp.concatenate([gh[:SUBLANES], g, gh[SUBLANES:]], axis=0)
        cw = cw_ref[j]
        gc = cb_ref[j] + g * cw[1:2]
        gc = gc + gp[SUBLANES - 1:SUBLANES - 1 + rows] * cw[0:1] + gp[SUBLANES + 1:SUBLANES + 1 + rows] * cw[2:3]
        act_ref[...] = (gc * _sigmoid(gc) * u).astype(BF16)

    def finish(j, act_ref):
        o_ref[...] += _dot(act_ref[...], wd_ref[pl.ds(pl.multiple_of(j * FF_CHUNK, FF_CHUNK), FF_CHUNK), :])

    def body(m, carry):
        j = 2 * m + 1
        start(j, act_b)
        finish(j - 1, act_a)
        start(j + 1, act_a)
        finish(j, act_b)
        return carry

    assert nck % 2 == 1
    start(0, act_a)
    lax.fori_loop(0, nck // 2, body, 0)
    finish(nck - 1, act_a)
    if final:
        o_ref[...] = _rms(o_ref[...], fw_ref[...])


def _ffn(x2, nw, wup, cw, cb, wd, fw, batch, seq, final):
    n = batch * seq
    nck = D_FF // FF_CHUNK
    rows = FF_ROWS if seq % FF_ROWS == 0 else seq
    hb = rows // SUBLANES
    return pl.pallas_call(
        functools.partial(_ffn_kernel, rows=rows, tiles_per_seq=seq // rows, final=final),
        grid=(n // rows,),
        in_specs=[
            pl.BlockSpec((rows, D_MODEL), lambda i: (i, 0)),
            pl.BlockSpec((SUBLANES, D_MODEL), lambda i: (jnp.maximum(i * hb - 1, 0), 0)),
            pl.BlockSpec((SUBLANES, D_MODEL), lambda i: (jnp.minimum((i + 1) * hb, n // SUBLANES - 1), 0)),
            pl.BlockSpec((1, D_MODEL), lambda i: (0, 0)),
            pl.BlockSpec((2 * nck, D_MODEL, FF_CHUNK), lambda i: (0, 0, 0)),
            pl.BlockSpec((nck, FFN_CONV, FF_CHUNK), lambda i: (0, 0, 0)),
            pl.BlockSpec((nck, 1, FF_CHUNK), lambda i: (0, 0, 0)),
            pl.BlockSpec((D_FF, D_MODEL), lambda i: (0, 0)),
            pl.BlockSpec((1, D_MODEL), lambda i: (0, 0)),
        ],
        out_specs=pl.BlockSpec((rows, D_MODEL), lambda i: (i, 0)),
        out_shape=jax.ShapeDtypeStruct((n, D_MODEL), F32),
        scratch_shapes=[pltpu.VMEM((rows, D_MODEL), BF16),
                        pltpu.VMEM((rows, FF_CHUNK), BF16),
                        pltpu.VMEM((rows, FF_CHUNK), BF16)],
        compiler_params=_cparams("parallel"),
        name="ffn_final" if final else "ffn",
    )(x2, x2, x2, nw, wup, cw, cb, wd, fw)


def _t5_bucket(rel):
    half = REL_BUCKETS // 2
    max_exact = half // 2
    ret = jnp.where(rel > 0, half, 0)
    n = jnp.abs(rel)
    nf = jnp.maximum(n, 1).astype(F32)
    large = max_exact + (jnp.log(nf / max_exact) / math.log(REL_MAX_DIST / max_exact)
                         * (half - max_exact)).astype(jnp.int32)
    large = jnp.minimum(large, half - 1)
    return ret + jnp.where(n < max_exact, n, large)


def _band_bias_t(rel_bias):
    rel = jnp.arange(KEY_SPAN)[:, None] - WINDOW - jnp.arange(QBLK)[None, :]
    onehot = (_t5_bucket(rel)[..., None] == jnp.arange(REL_BUCKETS)).astype(F32)
    bias = jnp.einsum("jib,bh->hji", onehot, rel_bias.astype(F32), precision=lax.Precision.HIGHEST)
    bias = jnp.where((jnp.abs(rel) <= WINDOW)[None], bias, -jnp.inf)
    return jnp.concatenate([bias, jnp.full((ATTN_HEADS, QBLK, QBLK), -jnp.inf, F32)], axis=1)


def _head_lanes(p):
    q = p.astype(F32).reshape(2, SSM_GROUPS, GROUP_HEADS).transpose(1, 0, 2).reshape(SSM_GROUPS, 2 * GROUP_HEADS)
    return jnp.pad(q, ((0, 0), (0, LANES - 2 * GROUP_HEADS)))[:, None, :]


def _expand_matrix():
    j = jnp.arange(LANES)[:, None]
    col = jnp.arange(2 * GROUP_W)[None, :]
    return jnp.where((j < 2 * GROUP_HEADS) & (col // HEAD_DIM == j), 1.0, 0.0).astype(BF16)


def kernel(x, rel_bias, norm1_w, w_in, conv_w, conv_b, dt_bias, a_log, d_skip, ssm_norm_w, attn_sink,
           w_out, norm2_w, w_up, ffn_conv_w, ffn_conv_b, w_down, final_norm_w):
    batch, seq, _ = x.shape
    assert seq % CHUNK == 0 and seq % QBLK == 0
    n = batch * seq
    depth = w_in.shape[0]
    x2 = x.reshape(n, D_MODEL)
    bias_t = _band_bias_t(rel_bias)
    e_mat = _expand_matrix()
    zw, xw_end = 1024, 2560
    dt_end = xw_end + 2 * SSM_HEADS
    q_end = dt_end + ATTN_HEADS * HEAD_DIM
    scale = HEAD_DIM ** -0.5
    nck = D_FF // FF_CHUNK

    for i in range(depth):
        wi = w_in[i]
        w_main = jnp.concatenate(
            [wi[:, :zw], wi[:, dt_end:q_end] * scale, wi[:, zw:xw_end], wi[:, q_end:]], axis=1).astype(BF16)
        wdt = wi[:, xw_end:dt_end].reshape(D_MODEL, 2, SSM_GROUPS, GROUP_HEADS).transpose(2, 0, 1, 3)
        wdt = jnp.pad(wdt.reshape(SSM_GROUPS, D_MODEL, 2 * GROUP_HEADS),
                      ((0, 0), (0, 0), (0, LANES - 2 * GROUP_HEADS))).astype(BF16)
        proj, dt = _inproj(x2, norm1_w[i][None], w_main, wdt)
        y_ssm = _ssd(proj, dt, conv_w[i], conv_b[i][None], _head_lanes(dt_bias[i]), _head_lanes(a_log[i]),
                     jnp.repeat(d_skip[i].astype(F32), HEAD_DIM)[None], e_mat, batch, seq)
        y_attn = _attn(proj, bias_t, attn_sink[i].astype(F32), batch, seq)
        wo = w_out[i].astype(BF16)
        x2 = _outproj(x2, y_ssm, proj, y_attn, ssm_norm_w[i][None], wo[:D_MODEL], wo[D_MODEL:])
        wup = w_up[i].astype(BF16).reshape(D_MODEL, 2 * nck, FF_CHUNK).transpose(1, 0, 2)
        fcw = ffn_conv_w[i].astype(F32).reshape(FFN_CONV, nck, FF_CHUNK).transpose(1, 0, 2)
        fcb = ffn_conv_b[i].astype(F32).reshape(nck, 1, FF_CHUNK)
        x2 = _ffn(x2, norm2_w[i][None], wup, fcw, fcb,
                  w_down[i].astype(BF16), final_norm_w[None], batch, seq, final=(i == depth - 1))

    return x2.reshape(batch, seq, D_MODEL)
```

```python
import functools
import math

import jax
import jax.numpy as jnp
from jax import lax
from jax.experimental import pallas as pl
from jax.experimental.pallas import tpu as pltpu

F32 = jnp.float32
BF16 = jnp.bfloat16

D_MODEL = 1024
HEAD_DIM = 64
SSM_HEADS = 16
SSM_GROUPS = 2
GROUP_HEADS = SSM_HEADS // SSM_GROUPS
GROUP_W = GROUP_HEADS * HEAD_DIM
SSM_STATE = 128
SSM_CONV = 7
CHUNK = 128
ATTN_HEADS = 16
KV_HEADS = 4
KV_REP = ATTN_HEADS // KV_HEADS
WINDOW = 128
QBLK = 128
KEY_SPAN = QBLK + 2 * WINDOW
KEY_TILES = KEY_SPAN // QBLK
REL_BUCKETS = 32
REL_MAX_DIST = 128
D_FF = 2816
FFN_CONV = 3
FF_CHUNK = 256
EPS = 1e-6
LOG2E = math.log2(math.e)
LANES = 128
SUBLANES = 8
HALO_ROWS = 2 * SUBLANES
MXU_SHIFT_TAPS = (6,)

COL_Z, COL_Q, COL_XS, COL_B, COL_C, COL_K, COL_V = 0, 1024, 2048, 3072, 3328, 3584, 3840
PROJ_COLS = 4096

VMEM_LIMIT = 56 * 1024 * 1024


def _cparams(*sem):
    return pltpu.CompilerParams(dimension_semantics=sem, vmem_limit_bytes=VMEM_LIMIT)


def _sigmoid(x):
    return 1.0 / (1.0 + jnp.exp(-x))


def _softplus(x):
    return jnp.maximum(x, 0.0) + jnp.log1p(jnp.exp(-jnp.abs(x)))


def _rms(x, w):
    ms = jnp.mean(x * x, axis=-1, keepdims=True)
    return x * lax.rsqrt(ms + EPS) * w


def _split3(x):
    hi = x.astype(BF16)
    r1 = x - hi.astype(F32)
    mid = r1.astype(BF16)
    lo = (r1 - mid.astype(F32)).astype(BF16)
    return hi, mid, lo


def _dot(a, b):
    return jnp.dot(a, b, preferred_element_type=F32)


def _dot_nt(a, b):
    return lax.dot_general(a, b, (((1,), (1,)), ((), ())), preferred_element_type=F32)


def _inproj_kernel(x_ref, nw_ref, w_ref, wdt_ref, proj_ref, dt_ref, *, tn):
    hb = _rms(x_ref[...], nw_ref[...]).astype(BF16)
    for g in range(SSM_GROUPS):
        dt_ref[g] = _dot(hb, wdt_ref[g])
    for j in range(PROJ_COLS // tn):
        proj_ref[:, j * tn:(j + 1) * tn] = _dot(hb, w_ref[:, j * tn:(j + 1) * tn]).astype(BF16)


def _inproj(x2, nw, w, wdt, tm=1024, tn=1024):
    n = x2.shape[0]
    return pl.pallas_call(
        functools.partial(_inproj_kernel, tn=tn),
        grid=(n // tm,),
        in_specs=[
            pl.BlockSpec((tm, D_MODEL), lambda i: (i, 0)),
            pl.BlockSpec((1, D_MODEL), lambda i: (0, 0)),
            pl.BlockSpec((D_MODEL, PROJ_COLS), lambda i: (0, 0)),
            pl.BlockSpec((SSM_GROUPS, D_MODEL, LANES), lambda i: (0, 0, 0)),
        ],
        out_specs=[
            pl.BlockSpec((tm, PROJ_COLS), lambda i: (i, 0)),
            pl.BlockSpec((SSM_GROUPS, tm, LANES), lambda i: (0, i, 0)),
        ],
        out_shape=[
            jax.ShapeDtypeStruct((n, PROJ_COLS), BF16),
            jax.ShapeDtypeStruct((SSM_GROUPS, n, LANES), F32),
        ],
        compiler_params=_cparams("parallel"),
        name="inproj",
    )(x2, nw, w, wdt)


def _ssd_kernel(xs_ref, b_ref, c_ref, dt_ref, cwx_ref, cwb_ref, cwc_ref, cbx_ref, cbb_ref,
                cbc_ref, dtb_ref, alog_ref, dskip_ref, e_ref, o_ref,
                xs_s, bt_s, c_s, cs_s, ut_s, cb_s, dec_s, sf_s, sb_s, ew_r, mix_r, *, seq):
    nchunk = seq // CHUNK
    npair = GROUP_W // LANES
    lane = lax.broadcasted_iota(jnp.int32, (CHUNK, LANES), 1)
    row = lax.broadcasted_iota(jnp.int32, (CHUNK, LANES), 0)
    is_fwd_lane = lane < GROUP_HEADS
    lo_half = lane < HEAD_DIM
    lt = lane < row
    gt = lane > row
    row2 = lax.broadcasted_iota(jnp.int32, (2 * CHUNK, LANES), 0)
    lane2 = lax.broadcasted_iota(jnp.int32, (2 * CHUNK, LANES), 1)
    tri2 = jnp.where(row2 < CHUNK, jnp.where(lane2 <= row2, 1.0, 0.0),
                     jnp.where(lane2 >= row2 - CHUNK, 1.0, 0.0)).astype(BF16)

    half = SSM_CONV // 2
    win_rows = CHUNK + 2 * HALO_ROWS
    mxu_taps = MXU_SHIFT_TAPS
    if mxu_taps:
        srow = lax.broadcasted_iota(jnp.int32, (len(mxu_taps) * CHUNK, win_rows), 0)
        scol = lax.broadcasted_iota(jnp.int32, (len(mxu_taps) * CHUNK, win_rows), 1)
        sblk = jnp.right_shift(srow, CHUNK.bit_length() - 1)
        soff = sum(jnp.where(sblk == i, HALO_ROWS - half + k, 0) for i, k in enumerate(mxu_taps))
        shift_mat = jnp.where(scol == srow - sblk * CHUNK + soff, 1.0, 0.0).astype(BF16)

    def conv_silu(src, w_ref, bias_ref, c, r0):
        lo = pl.multiple_of(jnp.maximum(r0 - HALO_ROWS, 0), HALO_ROWS)
        hi = pl.multiple_of(jnp.minimum(r0 + CHUNK, seq - HALO_ROWS), HALO_ROWS)
        prev = jnp.where(c > 0, src[pl.ds(lo, HALO_ROWS), :].astype(F32), 0.0)
        nxt = jnp.where(c < nchunk - 1, src[pl.ds(hi, HALO_ROWS), :].astype(F32), 0.0)
        cur = src[pl.ds(r0, CHUNK), :]
        curf = cur.astype(F32)
        if mxu_taps:
            win = jnp.concatenate([prev.astype(BF16), cur, nxt.astype(BF16)], axis=0)
            shifted = _dot(shift_mat, win)
        winf = jnp.concatenate([prev, curf, nxt], axis=0)
        acc = bias_ref[...] + curf * w_ref[half:half + 1, :]
        for i, k in enumerate(mxu_taps):
            acc = acc + shifted[i * CHUNK:(i + 1) * CHUNK] * w_ref[k:k + 1, :]
        for k in range(SSM_CONV):
            if k != half and k not in mxu_taps:
                off = HALO_ROWS - half + k
                acc = acc + winf[off:off + CHUNK] * w_ref[k:k + 1, :]
        return acc * _sigmoid(acc)

    a2_row = -jnp.exp(alog_ref[...]) * LOG2E
    e_mat = e_ref[...]

    def expand(q, exact=False):
        hi = q.astype(BF16)
        out = _dot(hi, e_mat)
        if exact:
            out = out + _dot((q - hi.astype(F32)).astype(BF16), e_mat)
        return out

    xs_s[0:CHUNK, :] = jnp.zeros((CHUNK, GROUP_W), F32)
    bt_s[0] = jnp.zeros((SSM_STATE, CHUNK), BF16)
    c_s[0:CHUNK, :] = jnp.zeros((CHUNK, SSM_STATE), BF16)
    ew_r[...] = jnp.zeros_like(ew_r)

    def prep_finish(cf, slot):
        rf = pl.multiple_of(cf * CHUNK, CHUNK)
        ewf = ew_r[slot]
        xf = xs_s[pl.ds(rf, CHUNK), :]
        btf = bt_s[cf]
        sf_s[cf + 1] = _dot(btf, (xf * ewf[:, :GROUP_W]).astype(BF16))
        sb_s[cf] = _dot(btf, (xf * ewf[:, GROUP_W:]).astype(BF16))
        cb_s[cf] = _dot(c_s[pl.ds(rf, CHUNK), :], btf)

    def prep_body(c, carry):
        prep_finish(jnp.maximum(c - 1, 0), (c + 1) % 2)
        r0 = pl.multiple_of(c * CHUNK, CHUNK)
        xs_s[pl.ds(r0, CHUNK), :] = conv_silu(xs_ref, cwx_ref, cbx_ref, c, r0)
        bt_s[c] = conv_silu(b_ref, cwb_ref, cbb_ref, c, r0).T.astype(BF16)
        c_s[pl.ds(r0, CHUNK), :] = conv_silu(c_ref, cwc_ref, cbc_ref, c, r0).astype(BF16)

        dt = _softplus(dt_ref[pl.ds(r0, CHUNK), :] + dtb_ref[...])
        hi, mid, lo = _split3(dt * a2_row)
        cums = _dot(tri2, hi) + _dot(tri2, mid) + _dot(tri2, lo)
        cum, rcum = cums[:CHUNK], cums[CHUNK:]
        cs = jnp.where(is_fwd_lane, cum, rcum)
        tot = jnp.where(is_fwd_lane[:1], cum[CHUNK - 1:CHUNK], rcum[0:1])
        cs_s[pl.ds(r0, CHUNK), :] = cs
        dsum = dt + pltpu.roll(dt, LANES - GROUP_HEADS, axis=1)
        ut_s[c, 0:2 * GROUP_HEADS, :] = (jnp.log2(dt) - cs).T[0:2 * GROUP_HEADS]
        ut_s[c, 2 * GROUP_HEADS:3 * GROUP_HEADS, :] = jnp.log2(dsum).T[0:GROUP_HEADS]
        dec_s[c] = expand(jnp.exp2(jnp.broadcast_to(tot, (SUBLANES, LANES))), exact=True)
        ew_r[c % 2] = expand(jnp.exp2(tot - cs) * dt)
        return carry

    lax.fori_loop(0, nchunk, prep_body, 0)
    prep_finish(nchunk - 1, (nchunk - 1) % 2)

    zstate = jnp.zeros((SSM_STATE, GROUP_W), F32)
    sf_s[0] = zstate
    sb_s[nchunk] = zstate

    def fwd_scan(c, carry):
        sf_s[c] = sf_s[c - 1] * dec_s[c - 1][0:1, :GROUP_W] + sf_s[c]
        return carry

    def bwd_scan(i, carry):
        c = nchunk - 2 - i
        sb_s[c + 1] = sb_s[c + 2] * dec_s[c + 1][0:1, GROUP_W:] + sb_s[c + 1]
        return carry

    lax.fori_loop(1, nchunk, fwd_scan, 0)
    lax.fori_loop(0, nchunk - 1, bwd_scan, 0)

    dskip = dskip_ref[...]
    mix_r[...] = jnp.zeros_like(mix_r)

    def out_finish(cf, slot):
        rf = pl.multiple_of(cf * CHUNK, CHUNK)
        x32 = xs_s[pl.ds(rf, CHUNK), :]
        cc = c_s[pl.ds(rf, CHUNK), :]
        ys = []
        for j in range(npair):
            xp = x32[:, j * LANES:(j + 1) * LANES]
            ys.append(_dot(mix_r[slot, 2 * j], jnp.where(lo_half, xp, 0.0).astype(BF16))
                      + _dot(mix_r[slot, 2 * j + 1], jnp.where(lo_half, 0.0, xp).astype(BF16)))
        y = jnp.concatenate(ys, axis=1)
        ee = expand(jnp.exp2(cs_s[pl.ds(rf, CHUNK), :]))
        y = (y + _dot(cc, sf_s[cf].astype(BF16)) * ee[:, :GROUP_W]
             + _dot(cc, sb_s[cf + 1].astype(BF16)) * ee[:, GROUP_W:])
        o_ref[pl.ds(rf, CHUNK), :] = (y + dskip * x32).astype(BF16)

    def out_body(c, carry):
        out_finish(jnp.maximum(c - 1, 0), (c + 1) % 2)
        r0 = pl.multiple_of(c * CHUNK, CHUNK)
        cs = cs_s[pl.ds(r0, CHUNK), :]
        cb = cb_s[c]
        for r in range(GROUP_HEADS):
            rb = GROUP_HEADS + r
            arg_f = cs[:, r:r + 1] + ut_s[c, r:r + 1, :]
            arg_b = cs[:, rb:rb + 1] + ut_s[c, rb:rb + 1, :]
            arg = jnp.where(lt, arg_f, jnp.where(gt, arg_b, ut_s[c, rb + GROUP_HEADS:rb + GROUP_HEADS + 1, :]))
            mix_r[c % 2, r] = (cb * jnp.exp2(arg)).astype(BF16)
        return carry

    lax.fori_loop(0, nchunk, out_body, 0, unroll=2)
    out_finish(nchunk - 1, (nchunk - 1) % 2)


def _ssd(proj, dt, cw, cb, dtb, alog, dskip, e_mat, batch, seq):
    n = batch * seq
    nchunk = seq // CHUNK
    wb = GROUP_W // LANES
    return pl.pallas_call(
        functools.partial(_ssd_kernel, seq=seq),
        grid=(batch, SSM_GROUPS),
        in_specs=[
            pl.BlockSpec((seq, GROUP_W), lambda b, g: (b, COL_XS // GROUP_W + g)),
            pl.BlockSpec((seq, LANES), lambda b, g: (b, COL_B // LANES + g)),
            pl.BlockSpec((seq, LANES), lambda b, g: (b, COL_C // LANES + g)),
            pl.BlockSpec((None, seq, LANES), lambda b, g: (g, b, 0)),
            pl.BlockSpec((SSM_CONV, GROUP_W), lambda b, g: (0, g)),
            pl.BlockSpec((SSM_CONV, LANES), lambda b, g: (0, SSM_GROUPS * wb + g)),
            pl.BlockSpec((SSM_CONV, LANES), lambda b, g: (0, SSM_GROUPS * wb + SSM_GROUPS + g)),
            pl.BlockSpec((1, GROUP_W), lambda b, g: (0, g)),
            pl.BlockSpec((1, LANES), lambda b, g: (0, SSM_GROUPS * wb + g)),
            pl.BlockSpec((1, LANES), lambda b, g: (0, SSM_GROUPS * wb + SSM_GROUPS + g)),
            pl.BlockSpec((None, 1, LANES), lambda b, g: (g, 0, 0)),
            pl.BlockSpec((None, 1, LANES), lambda b, g: (g, 0, 0)),
            pl.BlockSpec((1, GROUP_W), lambda b, g: (0, g)),
            pl.BlockSpec((LANES, 2 * GROUP_W), lambda b, g: (0, 0)),
        ],
        out_specs=pl.BlockSpec((seq, GROUP_W), lambda b, g: (b, g)),
        out_shape=jax.ShapeDtypeStruct((n, SSM_GROUPS * GROUP_W), BF16),
        scratch_shapes=[
            pltpu.VMEM((seq, GROUP_W), F32),
            pltpu.VMEM((nchunk, SSM_STATE, CHUNK), BF16),
            pltpu.VMEM((seq, SSM_STATE), BF16),
            pltpu.VMEM((seq, LANES), F32),
            pltpu.VMEM((nchunk, 3 * GROUP_HEADS, LANES), F32),
            pltpu.VMEM((nchunk, CHUNK, CHUNK), F32),
            pltpu.VMEM((nchunk, SUBLANES, 2 * GROUP_W), F32),
            pltpu.VMEM((nchunk + 1, SSM_STATE, GROUP_W), F32),
            pltpu.VMEM((nchunk + 1, SSM_STATE, GROUP_W), F32),
            pltpu.VMEM((2, CHUNK, 2 * GROUP_W), F32),
            pltpu.VMEM((2, GROUP_HEADS, CHUNK, CHUNK), BF16),
        ],
        compiler_params=_cparams("parallel", "parallel"),
        name="ssd",
    )(proj, proj, proj, dt, cw, cw, cw, cb, cb, cb, dtb, alog, dskip, e_mat)


def _attn_kernel(sink_ref, q_ref, k_ref, v_ref, bias_ref, o_ref, ke_s, ko_s, vt_s, *, seq):
    nblk = seq // QBLK
    kvw = KV_HEADS * HEAD_DIM
    ncol = kvw // LANES
    lane = lax.broadcasted_iota(jnp.int32, (QBLK, LANES), 1)
    lo_half = lane < HEAD_DIM

    zrow = jnp.zeros((WINDOW, KV_HEADS * LANES), BF16)
    for s in (ke_s, ko_s):
        s[0:WINDOW, :] = zrow
        s[WINDOW + seq:WINDOW + seq + WINDOW, :] = zrow
    zcol = jnp.zeros((kvw, WINDOW), BF16)
    vt_s[0] = zcol
    vt_s[nblk + 1] = zcol

    def fill_body(c, carry):
        r0 = pl.multiple_of(c * QBLK, QBLK)
        dst = pl.ds(WINDOW + r0, QBLK)
        kc = k_ref[pl.ds(r0, QBLK), :].astype(F32)
        for j in range(ncol):
            col = kc[:, j * LANES:(j + 1) * LANES]
            swp = pltpu.roll(col, HEAD_DIM, axis=1)
            g0, g1 = 2 * j, 2 * j + 1
            ke_s[dst, g0 * LANES:(g0 + 1) * LANES] = jnp.where(lo_half, col, 0.0).astype(BF16)
            ko_s[dst, g0 * LANES:(g0 + 1) * LANES] = jnp.where(lo_half, 0.0, swp).astype(BF16)
            ke_s[dst, g1 * LANES:(g1 + 1) * LANES] = jnp.where(lo_half, swp, 0.0).astype(BF16)
            ko_s[dst, g1 * LANES:(g1 + 1) * LANES] = jnp.where(lo_half, 0.0, col).astype(BF16)
        vt_s[c + 1] = v_ref[pl.ds(r0, QBLK), :].astype(F32).T.astype(BF16)
        return carry

    lax.fori_loop(0, nblk, fill_body, 0)

    def body(n, carry):
        r0 = pl.multiple_of(n * QBLK, QBLK)
        starts = (pl.multiple_of(jnp.where(n == 0, KEY_SPAN, 0), QBLK),
                  QBLK,
                  pl.multiple_of(jnp.where(n == nblk - 1, KEY_SPAN, KEY_SPAN - QBLK), QBLK))
        outs = []
        for pr in range(ATTN_HEADS // 2):
            qp = q_ref[pl.ds(r0, QBLK), pr * LANES:(pr + 1) * LANES]
            g = (2 * pr) // KV_REP
            vts = [vt_s[n + w, g * HEAD_DIM:(g + 1) * HEAD_DIM, :] for w in range(KEY_TILES)]
            for par, ks in ((0, ke_s), (1, ko_s)):
                h = 2 * pr + par
                sk = sink_ref[h]
                bias = jnp.concatenate([bias_ref[h, pl.ds(starts[w], QBLK), :] for w in range(KEY_TILES)], axis=0)
                t = _dot_nt(ks[pl.ds(r0, KEY_SPAN), g * LANES:(g + 1) * LANES], qp) + bias
                m = jnp.maximum(jnp.max(t, axis=0, keepdims=True), sk)
                p = jnp.exp(t - m)
                denom = jnp.sum(p, axis=0, keepdims=True) + jnp.exp(sk - m)
                pb = p.astype(BF16)
                pv = sum(_dot(vts[w], pb[w * QBLK:(w + 1) * QBLK]) for w in range(KEY_TILES))
                outs.append(pv * (1.0 / denom))
        o_ref[pl.ds(r0, QBLK), :] = jnp.concatenate(outs, axis=0).T.astype(BF16)
        return carry

    lax.fori_loop(0, nblk, body, 0, unroll=4)


def _attn(proj, bias_t, sink, batch, seq):
    n = batch * seq
    aw = ATTN_HEADS * HEAD_DIM
    kvw = KV_HEADS * HEAD_DIM
    return pl.pallas_call(
        functools.partial(_attn_kernel, seq=seq),
        grid=(batch,),
        in_specs=[
            pl.BlockSpec(memory_space=pltpu.SMEM),
            pl.BlockSpec((seq, aw), lambda b: (b, COL_Q // aw)),
            pl.BlockSpec((seq, kvw), lambda b: (b, COL_K // kvw)),
            pl.BlockSpec((seq, kvw), lambda b: (b, COL_V // kvw)),
            pl.BlockSpec((ATTN_HEADS, KEY_SPAN + QBLK, QBLK), lambda b: (0, 0, 0)),
        ],
        out_specs=pl.BlockSpec((seq, aw), lambda b: (b, 0)),
        out_shape=jax.ShapeDtypeStruct((n, aw), BF16),
        scratch_shapes=[
            pltpu.VMEM((seq + 2 * WINDOW, KV_HEADS * LANES), BF16),
            pltpu.VMEM((seq + 2 * WINDOW, KV_HEADS * LANES), BF16),
            pltpu.VMEM((seq // QBLK + 2, kvw, QBLK), BF16),
        ],
        compiler_params=_cparams("parallel"),
        name="attn",
    )(sink, proj, proj, proj, bias_t)


def _outproj_kernel(x_ref, ys_ref, z_ref, ya_ref, nw_ref, ws_ref, wa_ref, o_ref):
    z = z_ref[...].astype(F32)
    yg = ys_ref[...].astype(F32) * (z * _sigmoid(z))
    ysn = jnp.concatenate(
        [_rms(yg[:, g * GROUP_W:(g + 1) * GROUP_W], nw_ref[:, g * GROUP_W:(g + 1) * GROUP_W]).astype(BF16)
         for g in range(SSM_GROUPS)], axis=1)
    o_ref[...] = x_ref[...] + _dot(ysn, ws_ref[...]) + _dot(ya_ref[...], wa_ref[...])


def _outproj(x2, ys, proj, ya, nw, ws, wa, tm=1024):
    n = x2.shape[0]
    return pl.pallas_call(
        _outproj_kernel,
        grid=(n // tm,),
        in_specs=[
            pl.BlockSpec((tm, D_MODEL), lambda i: (i, 0)),
            pl.BlockSpec((tm, D_MODEL), lambda i: (i, 0)),
            pl.BlockSpec((tm, D_MODEL), lambda i: (i, COL_Z // D_MODEL)),
            pl.BlockSpec((tm, D_MODEL), lambda i: (i, 0)),
            pl.BlockSpec((1, D_MODEL), lambda i: (0, 0)),
            pl.BlockSpec((D_MODEL, D_MODEL), lambda i: (0, 0)),
            pl.BlockSpec((D_MODEL, D_MODEL), lambda i: (0, 0)),
        ],
        out_specs=pl.BlockSpec((tm, D_MODEL), lambda i: (i, 0)),
        out_shape=jax.ShapeDtypeStruct((n, D_MODEL), F32),
        compiler_params=_cparams("parallel"),
        name="outproj",
    )(x2, ys, proj, ya, nw, ws, wa)


def _ffn_kernel(x_ref, nw_ref, wg_ref, wu_ref, cw_ref, cb_ref, wd_ref, fw_ref, o_ref, h_ref, *, seq, final):
    j = pl.program_id(1)

    @pl.when(j == 0)
    def _():
        x = x_ref[...]
        h_ref[...] = _rms(x, nw_ref[...]).astype(BF16)
        o_ref[...] = x

    h = h_ref[...]
    g = _dot(h, wg_ref[...])
    u = _dot(h, wu_ref[...])
    cw = cw_ref[...]
    zpad = jnp.zeros((SUBLANES, FF_CHUNK), F32)
    gp = jnp.concatenate([zpad, g, zpad], axis=0)
    gc = cb_ref[...] + g * cw[1:2]
    gc = gc + gp[SUBLANES - 1:SUBLANES - 1 + seq] * cw[0:1] + gp[SUBLANES + 1:SUBLANES + 1 + seq] * cw[2:3]
    act = (gc * _sigmoid(gc) * u).astype(BF16)
    o_ref[...] += _dot(act, wd_ref[...])

    if final:
        @pl.when(j == pl.num_programs(1) - 1)
        def _():
            o_ref[...] = _rms(o_ref[...], fw_ref[...])


def _ffn(x2, nw, wup, cw, cb, wd, fw, batch, seq, final):
    n = batch * seq
    nck = D_FF // FF_CHUNK
    return pl.pallas_call(
        functools.partial(_ffn_kernel, seq=seq, final=final),
        grid=(batch, nck),
        in_specs=[
            pl.BlockSpec((seq, D_MODEL), lambda b, j: (b, 0)),
            pl.BlockSpec((1, D_MODEL), lambda b, j: (0, 0)),
            pl.BlockSpec((None, D_MODEL, FF_CHUNK), lambda b, j: (j, 0, 0)),
            pl.BlockSpec((None, D_MODEL, FF_CHUNK), lambda b, j: (nck + j, 0, 0)),
            pl.BlockSpec((FFN_CONV, FF_CHUNK), lambda b, j: (0, j)),
            pl.BlockSpec((1, FF_CHUNK), lambda b, j: (0, j)),
            pl.BlockSpec((FF_CHUNK, D_MODEL), lambda b, j: (j, 0)),
            pl.BlockSpec((1, D_MODEL), lambda b, j: (0, 0)),
        ],
        out_specs=pl.BlockSpec((seq, D_MODEL), lambda b, j: (b, 0)),
        out_shape=jax.ShapeDtypeStruct((n, D_MODEL), F32),
        scratch_shapes=[pltpu.VMEM((seq, D_MODEL), BF16)],
        compiler_params=_cparams("parallel", "arbitrary"),
        name="ffn_final" if final else "ffn",
    )(x2, nw, wup, wup, cw, cb, wd, fw)


def _t5_bucket(rel):
    half = REL_BUCKETS // 2
    max_exact = half // 2
    ret = jnp.where(rel > 0, half, 0)
    n = jnp.abs(rel)
    nf = jnp.maximum(n, 1).astype(F32)
    large = max_exact + (jnp.log(nf / max_exact) / math.log(REL_MAX_DIST / max_exact)
                         * (half - max_exact)).astype(jnp.int32)
    large = jnp.minimum(large, half - 1)
    return ret + jnp.where(n < max_exact, n, large)


def _band_bias_t(rel_bias):
    rel = jnp.arange(KEY_SPAN)[:, None] - WINDOW - jnp.arange(QBLK)[None, :]
    onehot = (_t5_bucket(rel)[..., None] == jnp.arange(REL_BUCKETS)).astype(F32)
    bias = jnp.einsum("jib,bh->hji", onehot, rel_bias.astype(F32), precision=lax.Precision.HIGHEST)
    bias = jnp.where((jnp.abs(rel) <= WINDOW)[None], bias, -jnp.inf)
    return jnp.concatenate([bias, jnp.full((ATTN_HEADS, QBLK, QBLK), -jnp.inf, F32)], axis=1)


def _head_lanes(p):
    q = p.astype(F32).reshape(2, SSM_GROUPS, GROUP_HEADS).transpose(1, 0, 2).reshape(SSM_GROUPS, 2 * GROUP_HEADS)
    return jnp.pad(q, ((0, 0), (0, LANES - 2 * GROUP_HEADS)))[:, None, :]


def _expand_matrix():
    j = jnp.arange(LANES)[:, None]
    col = jnp.arange(2 * GROUP_W)[None, :]
    return jnp.where((j < 2 * GROUP_HEADS) & (col // HEAD_DIM == j), 1.0, 0.0).astype(BF16)


def kernel(x, rel_bias, norm1_w, w_in, conv_w, conv_b, dt_bias, a_log, d_skip, ssm_norm_w, attn_sink,
           w_out, norm2_w, w_up, ffn_conv_w, ffn_conv_b, w_down, final_norm_w):
    batch, seq, _ = x.shape
    assert seq % CHUNK == 0 and seq % QBLK == 0
    n = batch * seq
    depth = w_in.shape[0]
    x2 = x.reshape(n, D_MODEL)
    bias_t = _band_bias_t(rel_bias)
    e_mat = _expand_matrix()
    zw, xw_end = 1024, 2560
    dt_end = xw_end + 2 * SSM_HEADS
    q_end = dt_end + ATTN_HEADS * HEAD_DIM
    scale = HEAD_DIM ** -0.5
    nck = D_FF // FF_CHUNK

    for i in range(depth):
        wi = w_in[i]
        w_main = jnp.concatenate(
            [wi[:, :zw], wi[:, dt_end:q_end] * scale, wi[:, zw:xw_end], wi[:, q_end:]], axis=1).astype(BF16)
        wdt = wi[:, xw_end:dt_end].reshape(D_MODEL, 2, SSM_GROUPS, GROUP_HEADS).transpose(2, 0, 1, 3)
        wdt = jnp.pad(wdt.reshape(SSM_GROUPS, D_MODEL, 2 * GROUP_HEADS),
                      ((0, 0), (0, 0), (0, LANES - 2 * GROUP_HEADS))).astype(BF16)
        proj, dt = _inproj(x2, norm1_w[i][None], w_main, wdt)
        y_ssm = _ssd(proj, dt, conv_w[i], conv_b[i][None], _head_lanes(dt_bias[i]), _head_lanes(a_log[i]),
                     jnp.repeat(d_skip[i].astype(F32), HEAD_DIM)[None], e_mat, batch, seq)
        y_attn = _attn(proj, bias_t, attn_sink[i].astype(F32), batch, seq)
        wo = w_out[i].astype(BF16)
        x2 = _outproj(x2, y_ssm, proj, y_attn, ssm_norm_w[i][None], wo[:D_MODEL], wo[D_MODEL:])
        wup = w_up[i].astype(BF16).reshape(D_MODEL, 2 * nck, FF_CHUNK).transpose(1, 0, 2)
        x2 = _ffn(x2, norm2_w[i][None], wup, ffn_conv_w[i], ffn_conv_b[i][None],
                  w_down[i].astype(BF16), final_norm_w[None], batch, seq, final=(i == depth - 1))

    return x2.reshape(batch, seq, D_MODEL)
```

```python
import functools
import math

import jax
import jax.numpy as jnp
from jax import lax
from jax.experimental import pallas as pl
from jax.experimental.pallas import tpu as pltpu

F32 = jnp.float32
BF16 = jnp.bfloat16

D_MODEL = 1024
HEAD_DIM = 64
SSM_HEADS = 16
SSM_GROUPS = 2
GROUP_HEADS = SSM_HEADS // SSM_GROUPS
GROUP_W = GROUP_HEADS * HEAD_DIM
SSM_STATE = 128
SSM_CONV = 7
CHUNK = 128
ATTN_HEADS = 16
KV_HEADS = 4
KV_REP = ATTN_HEADS // KV_HEADS
WINDOW = 128
QBLK = 128
KEY_SPAN = QBLK + 2 * WINDOW
KEY_TILES = KEY_SPAN // QBLK
REL_BUCKETS = 32
REL_MAX_DIST = 128
D_FF = 2816
FFN_CONV = 3
FF_CHUNK = 256
EPS = 1e-6
LOG2E = math.log2(math.e)
LANES = 128
SUBLANES = 8
HALO_ROWS = 2 * SUBLANES
MXU_SHIFT_TAPS = (6,)

COL_Z, COL_Q, COL_XS, COL_B, COL_C, COL_K, COL_V = 0, 1024, 2048, 3072, 3328, 3584, 3840
PROJ_COLS = 4096

VMEM_LIMIT = 56 * 1024 * 1024


def _cparams(*sem):
    return pltpu.CompilerParams(dimension_semantics=sem, vmem_limit_bytes=VMEM_LIMIT)


def _sigmoid(x):
    return 1.0 / (1.0 + jnp.exp(-x))


def _softplus(x):
    return jnp.maximum(x, 0.0) + jnp.log1p(jnp.exp(-jnp.abs(x)))


def _rms(x, w):
    ms = jnp.mean(x * x, axis=-1, keepdims=True)
    return x * lax.rsqrt(ms + EPS) * w


def _split3(x):
    hi = x.astype(BF16)
    r1 = x - hi.astype(F32)
    mid = r1.astype(BF16)
    lo = (r1 - mid.astype(F32)).astype(BF16)
    return hi, mid, lo


def _dot(a, b):
    return jnp.dot(a, b, preferred_element_type=F32)


def _dot_nt(a, b):
    return lax.dot_general(a, b, (((1,), (1,)), ((), ())), preferred_element_type=F32)


def _inproj_kernel(x_ref, nw_ref, w_ref, wdt_ref, proj_ref, dt_ref, *, tn):
    hb = _rms(x_ref[...], nw_ref[...]).astype(BF16)
    for g in range(SSM_GROUPS):
        dt_ref[g] = _dot(hb, wdt_ref[g])
    for j in range(PROJ_COLS // tn):
        proj_ref[:, j * tn:(j + 1) * tn] = _dot(hb, w_ref[:, j * tn:(j + 1) * tn]).astype(BF16)


def _inproj(x2, nw, w, wdt, tm=1024, tn=1024):
    n = x2.shape[0]
    return pl.pallas_call(
        functools.partial(_inproj_kernel, tn=tn),
        grid=(n // tm,),
        in_specs=[
            pl.BlockSpec((tm, D_MODEL), lambda i: (i, 0)),
            pl.BlockSpec((1, D_MODEL), lambda i: (0, 0)),
            pl.BlockSpec((D_MODEL, PROJ_COLS), lambda i: (0, 0)),
            pl.BlockSpec((SSM_GROUPS, D_MODEL, LANES), lambda i: (0, 0, 0)),
        ],
        out_specs=[
            pl.BlockSpec((tm, PROJ_COLS), lambda i: (i, 0)),
            pl.BlockSpec((SSM_GROUPS, tm, LANES), lambda i: (0, i, 0)),
        ],
        out_shape=[
            jax.ShapeDtypeStruct((n, PROJ_COLS), BF16),
            jax.ShapeDtypeStruct((SSM_GROUPS, n, LANES), F32),
        ],
        compiler_params=_cparams("parallel"),
        name="inproj",
    )(x2, nw, w, wdt)


def _ssd_kernel(xs_ref, b_ref, c_ref, dt_ref, cwx_ref, cwb_ref, cwc_ref, cbx_ref, cbb_ref,
                cbc_ref, dtb_ref, alog_ref, dskip_ref, e_ref, o_ref,
                xs_s, bt_s, c_s, cs_s, ut_s, cb_s, dec_s, sf_s, sb_s, ew_r, mix_r, *, seq):
    nchunk = seq // CHUNK
    npair = GROUP_W // LANES
    lane = lax.broadcasted_iota(jnp.int32, (CHUNK, LANES), 1)
    row = lax.broadcasted_iota(jnp.int32, (CHUNK, LANES), 0)
    is_fwd_lane = lane < GROUP_HEADS
    lo_half = lane < HEAD_DIM
    lt = lane < row
    gt = lane > row
    row2 = lax.broadcasted_iota(jnp.int32, (2 * CHUNK, LANES), 0)
    lane2 = lax.broadcasted_iota(jnp.int32, (2 * CHUNK, LANES), 1)
    tri2 = jnp.where(row2 < CHUNK, jnp.where(lane2 <= row2, 1.0, 0.0),
                     jnp.where(lane2 >= row2 - CHUNK, 1.0, 0.0)).astype(BF16)

    half = SSM_CONV // 2
    win_rows = CHUNK + 2 * HALO_ROWS
    mxu_taps = MXU_SHIFT_TAPS
    if mxu_taps:
        srow = lax.broadcasted_iota(jnp.int32, (len(mxu_taps) * CHUNK, win_rows), 0)
        scol = lax.broadcasted_iota(jnp.int32, (len(mxu_taps) * CHUNK, win_rows), 1)
        sblk = jnp.right_shift(srow, CHUNK.bit_length() - 1)
        soff = sum(jnp.where(sblk == i, HALO_ROWS - half + k, 0) for i, k in enumerate(mxu_taps))
        shift_mat = jnp.where(scol == srow - sblk * CHUNK + soff, 1.0, 0.0).astype(BF16)

    def conv_silu(src, w_ref, bias_ref, c, r0):
        lo = pl.multiple_of(jnp.maximum(r0 - HALO_ROWS, 0), HALO_ROWS)
        hi = pl.multiple_of(jnp.minimum(r0 + CHUNK, seq - HALO_ROWS), HALO_ROWS)
        prev = jnp.where(c > 0, src[pl.ds(lo, HALO_ROWS), :].astype(F32), 0.0)
        nxt = jnp.where(c < nchunk - 1, src[pl.ds(hi, HALO_ROWS), :].astype(F32), 0.0)
        cur = src[pl.ds(r0, CHUNK), :]
        curf = cur.astype(F32)
        if mxu_taps:
            win = jnp.concatenate([prev.astype(BF16), cur, nxt.astype(BF16)], axis=0)
            shifted = _dot(shift_mat, win)
        winf = jnp.concatenate([prev, curf, nxt], axis=0)
        acc = bias_ref[...] + curf * w_ref[half:half + 1, :]
        for i, k in enumerate(mxu_taps):
            acc = acc + shifted[i * CHUNK:(i + 1) * CHUNK] * w_ref[k:k + 1, :]
        for k in range(SSM_CONV):
            if k != half and k not in mxu_taps:
                off = HALO_ROWS - half + k
                acc = acc + winf[off:off + CHUNK] * w_ref[k:k + 1, :]
        return acc * _sigmoid(acc)

    a2_row = -jnp.exp(alog_ref[...]) * LOG2E
    e_mat = e_ref[...]

    def expand(q, exact=False):
        hi = q.astype(BF16)
        out = _dot(hi, e_mat)
        if exact:
            out = out + _dot((q - hi.astype(F32)).astype(BF16), e_mat)
        return out

    xs_s[0:CHUNK, :] = jnp.zeros((CHUNK, GROUP_W), F32)
    bt_s[0] = jnp.zeros((SSM_STATE, CHUNK), BF16)
    c_s[0:CHUNK, :] = jnp.zeros((CHUNK, SSM_STATE), BF16)
    ew_r[...] = jnp.zeros_like(ew_r)

    def prep_finish(cf, slot):
        rf = pl.multiple_of(cf * CHUNK, CHUNK)
        ewf = ew_r[slot]
        xf = xs_s[pl.ds(rf, CHUNK), :]
        btf = bt_s[cf]
        sf_s[cf + 1] = _dot(btf, (xf * ewf[:, :GROUP_W]).astype(BF16))
        sb_s[cf] = _dot(btf, (xf * ewf[:, GROUP_W:]).astype(BF16))
        cb_s[cf] = _dot(c_s[pl.ds(rf, CHUNK), :], btf)

    def prep_body(c, carry):
        prep_finish(jnp.maximum(c - 1, 0), (c + 1) % 2)
        r0 = pl.multiple_of(c * CHUNK, CHUNK)
        xs_s[pl.ds(r0, CHUNK), :] = conv_silu(xs_ref, cwx_ref, cbx_ref, c, r0)
        bt_s[c] = conv_silu(b_ref, cwb_ref, cbb_ref, c, r0).T.astype(BF16)
        c_s[pl.ds(r0, CHUNK), :] = conv_silu(c_ref, cwc_ref, cbc_ref, c, r0).astype(BF16)

        dt = _softplus(dt_ref[pl.ds(r0, CHUNK), :] + dtb_ref[...])
        hi, mid, lo = _split3(dt * a2_row)
        cums = _dot(tri2, hi) + _dot(tri2, mid) + _dot(tri2, lo)
        cum, rcum = cums[:CHUNK], cums[CHUNK:]
        cs = jnp.where(is_fwd_lane, cum, rcum)
        tot = jnp.where(is_fwd_lane[:1], cum[CHUNK - 1:CHUNK], rcum[0:1])
        cs_s[pl.ds(r0, CHUNK), :] = cs
        dsum = dt + pltpu.roll(dt, LANES - GROUP_HEADS, axis=1)
        ut_s[c, 0:2 * GROUP_HEADS, :] = (jnp.log2(dt) - cs).T[0:2 * GROUP_HEADS]
        ut_s[c, 2 * GROUP_HEADS:3 * GROUP_HEADS, :] = jnp.log2(dsum).T[0:GROUP_HEADS]
        dec_s[c] = expand(jnp.exp2(jnp.broadcast_to(tot, (SUBLANES, LANES))), exact=True)
        ew_r[c % 2] = expand(jnp.exp2(tot - cs) * dt)
        return carry

    lax.fori_loop(0, nchunk, prep_body, 0)
    prep_finish(nchunk - 1, (nchunk - 1) % 2)

    zstate = jnp.zeros((SSM_STATE, GROUP_W), F32)
    sf_s[0] = zstate
    sb_s[nchunk] = zstate

    def fwd_scan(c, carry):
        sf_s[c] = sf_s[c - 1] * dec_s[c - 1][0:1, :GROUP_W] + sf_s[c]
        return carry

    def bwd_scan(i, carry):
        c = nchunk - 2 - i
        sb_s[c + 1] = sb_s[c + 2] * dec_s[c + 1][0:1, GROUP_W:] + sb_s[c + 1]
        return carry

    lax.fori_loop(1, nchunk, fwd_scan, 0)
    lax.fori_loop(0, nchunk - 1, bwd_scan, 0)

    dskip = dskip_ref[...]
    mix_r[...] = jnp.zeros_like(mix_r)

    def out_finish(cf, slot):
        rf = pl.multiple_of(cf * CHUNK, CHUNK)
        x32 = xs_s[pl.ds(rf, CHUNK), :]
        cc = c_s[pl.ds(rf, CHUNK), :]
        ys = []
        for j in range(npair):
            xp = x32[:, j * LANES:(j + 1) * LANES]
            ys.append(_dot(mix_r[slot, 2 * j], jnp.where(lo_half, xp, 0.0).astype(BF16))
                      + _dot(mix_r[slot, 2 * j + 1], jnp.where(lo_half, 0.0, xp).astype(BF16)))
        y = jnp.concatenate(ys, axis=1)
        ee = expand(jnp.exp2(cs_s[pl.ds(rf, CHUNK), :]))
        y = (y + _dot(cc, sf_s[cf].astype(BF16)) * ee[:, :GROUP_W]
             + _dot(cc, sb_s[cf + 1].astype(BF16)) * ee[:, GROUP_W:])
        o_ref[pl.ds(rf, CHUNK), :] = (y + dskip * x32).astype(BF16)

    def out_body(c, carry):
        out_finish(jnp.maximum(c - 1, 0), (c + 1) % 2)
        r0 = pl.multiple_of(c * CHUNK, CHUNK)
        cs = cs_s[pl.ds(r0, CHUNK), :]
        cb = cb_s[c]
        for r in range(GROUP_HEADS):
            rb = GROUP_HEADS + r
            arg_f = cs[:, r:r + 1] + ut_s[c, r:r + 1, :]
            arg_b = cs[:, rb:rb + 1] + ut_s[c, rb:rb + 1, :]
            arg = jnp.where(lt, arg_f, jnp.where(gt, arg_b, ut_s[c, rb + GROUP_HEADS:rb + GROUP_HEADS + 1, :]))
            mix_r[c % 2, r] = (cb * jnp.exp2(arg)).astype(BF16)
        return carry

    lax.fori_loop(0, nchunk, out_body, 0, unroll=2)
    out_finish(nchunk - 1, (nchunk - 1) % 2)


def _ssd(proj, dt, cw, cb, dtb, alog, dskip, e_mat, batch, seq):
    n = batch * seq
    nchunk = seq // CHUNK
    wb = GROUP_W // LANES
    return pl.pallas_call(
        functools.partial(_ssd_kernel, seq=seq),
        grid=(batch, SSM_GROUPS),
        in_specs=[
            pl.BlockSpec((seq, GROUP_W), lambda b, g: (b, COL_XS // GROUP_W + g)),
            pl.BlockSpec((seq, LANES), lambda b, g: (b, COL_B // LANES + g)),
            pl.BlockSpec((seq, LANES), lambda b, g: (b, COL_C // LANES + g)),
            pl.BlockSpec((None, seq, LANES), lambda b, g: (g, b, 0)),
            pl.BlockSpec((SSM_CONV, GROUP_W), lambda b, g: (0, g)),
            pl.BlockSpec((SSM_CONV, LANES), lambda b, g: (0, SSM_GROUPS * wb + g)),
            pl.BlockSpec((SSM_CONV, LANES), lambda b, g: (0, SSM_GROUPS * wb + SSM_GROUPS + g)),
            pl.BlockSpec((1, GROUP_W), lambda b, g: (0, g)),
            pl.BlockSpec((1, LANES), lambda b, g: (0, SSM_GROUPS * wb + g)),
            pl.BlockSpec((1, LANES), lambda b, g: (0, SSM_GROUPS * wb + SSM_GROUPS + g)),
            pl.BlockSpec((None, 1, LANES), lambda b, g: (g, 0, 0)),
            pl.BlockSpec((None, 1, LANES), lambda b, g: (g, 0, 0)),
            pl.BlockSpec((1, GROUP_W), lambda b, g: (0, g)),
            pl.BlockSpec((LANES, 2 * GROUP_W), lambda b, g: (0, 0)),
        ],
        out_specs=pl.BlockSpec((seq, GROUP_W), lambda b, g: (b, g)),
        out_shape=jax.ShapeDtypeStruct((n, SSM_GROUPS * GROUP_W), BF16),
        scratch_shapes=[
            pltpu.VMEM((seq, GROUP_W), F32),
            pltpu.VMEM((nchunk, SSM_STATE, CHUNK), BF16),
            pltpu.VMEM((seq, SSM_STATE), BF16),
            pltpu.VMEM((seq, LANES), F32),
            pltpu.VMEM((nchunk, 3 * GROUP_HEADS, LANES), F32),
            pltpu.VMEM((nchunk, CHUNK, CHUNK), F32),
            pltpu.VMEM((nchunk, SUBLANES, 2 * GROUP_W), F32),
            pltpu.VMEM((nchunk + 1, SSM_STATE, GROUP_W), F32),
            pltpu.VMEM((nchunk + 1, SSM_STATE, GROUP_W), F32),
            pltpu.VMEM((2, CHUNK, 2 * GROUP_W), F32),
            pltpu.VMEM((2, GROUP_HEADS, CHUNK, CHUNK), BF16),
        ],
        compiler_params=_cparams("parallel", "parallel"),
        name="ssd",
    )(proj, proj, proj, dt, cw, cw, cw, cb, cb, cb, dtb, alog, dskip, e_mat)


def _attn_kernel(sink_ref, q_ref, k_ref, v_ref, bias_ref, o_ref, ke_s, ko_s, vt_s, *, seq):
    nblk = seq // QBLK
    kvw = KV_HEADS * HEAD_DIM
    ncol = kvw // LANES
    lane = lax.broadcasted_iota(jnp.int32, (QBLK, LANES), 1)
    lo_half = lane < HEAD_DIM

    zrow = jnp.zeros((WINDOW, KV_HEADS * LANES), BF16)
    for s in (ke_s, ko_s):
        s[0:WINDOW, :] = zrow
        s[WINDOW + seq:WINDOW + seq + WINDOW, :] = zrow
    zcol = jnp.zeros((kvw, WINDOW), BF16)
    vt_s[0] = zcol
    vt_s[nblk + 1] = zcol

    def fill_body(c, carry):
        r0 = pl.multiple_of(c * QBLK, QBLK)
        dst = pl.ds(WINDOW + r0, QBLK)
        kc = k_ref[pl.ds(r0, QBLK), :].astype(F32)
        for j in range(ncol):
            col = kc[:, j * LANES:(j + 1) * LANES]
            swp = pltpu.roll(col, HEAD_DIM, axis=1)
            g0, g1 = 2 * j, 2 * j + 1
            ke_s[dst, g0 * LANES:(g0 + 1) * LANES] = jnp.where(lo_half, col, 0.0).astype(BF16)
            ko_s[dst, g0 * LANES:(g0 + 1) * LANES] = jnp.where(lo_half, 0.0, swp).astype(BF16)
            ke_s[dst, g1 * LANES:(g1 + 1) * LANES] = jnp.where(lo_half, swp, 0.0).astype(BF16)
            ko_s[dst, g1 * LANES:(g1 + 1) * LANES] = jnp.where(lo_half, 0.0, col).astype(BF16)
        vt_s[c + 1] = v_ref[pl.ds(r0, QBLK), :].astype(F32).T.astype(BF16)
        return carry

    lax.fori_loop(0, nblk, fill_body, 0, unroll=4)

    def body(n, carry):
        r0 = pl.multiple_of(n * QBLK, QBLK)
        starts = (pl.multiple_of(jnp.where(n == 0, KEY_SPAN, 0), QBLK),
                  QBLK,
                  pl.multiple_of(jnp.where(n == nblk - 1, KEY_SPAN, KEY_SPAN - QBLK), QBLK))
        outs = []
        for pr in range(ATTN_HEADS // 2):
            qp = q_ref[pl.ds(r0, QBLK), pr * LANES:(pr + 1) * LANES]
            g = (2 * pr) // KV_REP
            vts = [vt_s[n + w, g * HEAD_DIM:(g + 1) * HEAD_DIM, :] for w in range(KEY_TILES)]
            for par, ks in ((0, ke_s), (1, ko_s)):
                h = 2 * pr + par
                sk = sink_ref[h]
                bias = jnp.concatenate([bias_ref[h, pl.ds(starts[w], QBLK), :] for w in range(KEY_TILES)], axis=0)
                t = _dot_nt(ks[pl.ds(r0, KEY_SPAN), g * LANES:(g + 1) * LANES], qp) + bias
                m = jnp.maximum(jnp.max(t, axis=0, keepdims=True), sk)
                p = jnp.exp(t - m)
                denom = jnp.sum(p, axis=0, keepdims=True) + jnp.exp(sk - m)
                pb = p.astype(BF16)
                pv = sum(_dot(vts[w], pb[w * QBLK:(w + 1) * QBLK]) for w in range(KEY_TILES))
                outs.append(pv * (1.0 / denom))
        o_ref[pl.ds(r0, QBLK), :] = jnp.concatenate(outs, axis=0).T.astype(BF16)
        return carry

    lax.fori_loop(0, nblk, body, 0, unroll=8)


def _attn(proj, bias_t, sink, batch, seq):
    n = batch * seq
    aw = ATTN_HEADS * HEAD_DIM
    kvw = KV_HEADS * HEAD_DIM
    return pl.pallas_call(
        functools.partial(_attn_kernel, seq=seq),
        grid=(batch,),
        in_specs=[
            pl.BlockSpec(memory_space=pltpu.SMEM),
            pl.BlockSpec((seq, aw), lambda b: (b, COL_Q // aw)),
            pl.BlockSpec((seq, kvw), lambda b: (b, COL_K // kvw)),
            pl.BlockSpec((seq, kvw), lambda b: (b, COL_V // kvw)),
            pl.BlockSpec((ATTN_HEADS, KEY_SPAN + QBLK, QBLK), lambda b: (0, 0, 0)),
        ],
        out_specs=pl.BlockSpec((seq, aw), lambda b: (b, 0)),
        out_shape=jax.ShapeDtypeStruct((n, aw), BF16),
        scratch_shapes=[
            pltpu.VMEM((seq + 2 * WINDOW, KV_HEADS * LANES), BF16),
            pltpu.VMEM((seq + 2 * WINDOW, KV_HEADS * LANES), BF16),
            pltpu.VMEM((seq // QBLK + 2, kvw, QBLK), BF16),
        ],
        compiler_params=_cparams("parallel"),
        name="attn",
    )(sink, proj, proj, proj, bias_t)


def _outproj_kernel(x_ref, ys_ref, z_ref, ya_ref, nw_ref, ws_ref, wa_ref, o_ref):
    z = z_ref[...].astype(F32)
    yg = ys_ref[...].astype(F32) * (z * _sigmoid(z))
    ysn = jnp.concatenate(
        [_rms(yg[:, g * GROUP_W:(g + 1) * GROUP_W], nw_ref[:, g * GROUP_W:(g + 1) * GROUP_W]).astype(BF16)
         for g in range(SSM_GROUPS)], axis=1)
    o_ref[...] = x_ref[...] + _dot(ysn, ws_ref[...]) + _dot(ya_ref[...], wa_ref[...])


def _outproj(x2, ys, proj, ya, nw, ws, wa, tm=1024):
    n = x2.shape[0]
    return pl.pallas_call(
        _outproj_kernel,
        grid=(n // tm,),
        in_specs=[
            pl.BlockSpec((tm, D_MODEL), lambda i: (i, 0)),
            pl.BlockSpec((tm, D_MODEL), lambda i: (i, 0)),
            pl.BlockSpec((tm, D_MODEL), lambda i: (i, COL_Z // D_MODEL)),
            pl.BlockSpec((tm, D_MODEL), lambda i: (i, 0)),
            pl.BlockSpec((1, D_MODEL), lambda i: (0, 0)),
            pl.BlockSpec((D_MODEL, D_MODEL), lambda i: (0, 0)),
            pl.BlockSpec((D_MODEL, D_MODEL), lambda i: (0, 0)),
        ],
        out_specs=pl.BlockSpec((tm, D_MODEL), lambda i: (i, 0)),
        out_shape=jax.ShapeDtypeStruct((n, D_MODEL), F32),
        compiler_params=_cparams("parallel"),
        name="outproj",
    )(x2, ys, proj, ya, nw, ws, wa)


def _ffn_kernel(x_ref, nw_ref, wg_ref, wu_ref, cw_ref, cb_ref, wd_ref, fw_ref, o_ref, h_ref, *, seq, final):
    j = pl.program_id(1)

    @pl.when(j == 0)
    def _():
        x = x_ref[...]
        h_ref[...] = _rms(x, nw_ref[...]).astype(BF16)
        o_ref[...] = x

    h = h_ref[...]
    g = _dot(h, wg_ref[...])
    u = _dot(h, wu_ref[...])
    cw = cw_ref[...]
    zpad = jnp.zeros((SUBLANES, FF_CHUNK), F32)
    gp = jnp.concatenate([zpad, g, zpad], axis=0)
    gc = cb_ref[...] + g * cw[1:2]
    gc = gc + gp[SUBLANES - 1:SUBLANES - 1 + seq] * cw[0:1] + gp[SUBLANES + 1:SUBLANES + 1 + seq] * cw[2:3]
    act = (gc * _sigmoid(gc) * u).astype(BF16)
    o_ref[...] += _dot(act, wd_ref[...])

    if final:
        @pl.when(j == pl.num_programs(1) - 1)
        def _():
            o_ref[...] = _rms(o_ref[...], fw_ref[...])


def _ffn(x2, nw, wup, cw, cb, wd, fw, batch, seq, final):
    n = batch * seq
    nck = D_FF // FF_CHUNK
    return pl.pallas_call(
        functools.partial(_ffn_kernel, seq=seq, final=final),
        grid=(batch, nck),
        in_specs=[
            pl.BlockSpec((seq, D_MODEL), lambda b, j: (b, 0)),
            pl.BlockSpec((1, D_MODEL), lambda b, j: (0, 0)),
            pl.BlockSpec((None, D_MODEL, FF_CHUNK), lambda b, j: (j, 0, 0)),
            pl.BlockSpec((None, D_MODEL, FF_CHUNK), lambda b, j: (nck + j, 0, 0)),
            pl.BlockSpec((FFN_CONV, FF_CHUNK), lambda b, j: (0, j)),
            pl.BlockSpec((1, FF_CHUNK), lambda b, j: (0, j)),
            pl.BlockSpec((FF_CHUNK, D_MODEL), lambda b, j: (j, 0)),
            pl.BlockSpec((1, D_MODEL), lambda b, j: (0, 0)),
        ],
        out_specs=pl.BlockSpec((seq, D_MODEL), lambda b, j: (b, 0)),
        out_shape=jax.ShapeDtypeStruct((n, D_MODEL), F32),
        scratch_shapes=[pltpu.VMEM((seq, D_MODEL), BF16)],
        compiler_params=_cparams("parallel", "arbitrary"),
        name="ffn_final" if final else "ffn",
    )(x2, nw, wup, wup, cw, cb, wd, fw)


def _t5_bucket(rel):
    half = REL_BUCKETS // 2
    max_exact = half // 2
    ret = jnp.where(rel > 0, half, 0)
    n = jnp.abs(rel)
    nf = jnp.maximum(n, 1).astype(F32)
    large = max_exact + (jnp.log(nf / max_exact) / math.log(REL_MAX_DIST / max_exact)
                         * (half - max_exact)).astype(jnp.int32)
    large = jnp.minimum(large, half - 1)
    return ret + jnp.where(n < max_exact, n, large)


def _band_bias_t(rel_bias):
    rel = jnp.arange(KEY_SPAN)[:, None] - WINDOW - jnp.arange(QBLK)[None, :]
    onehot = (_t5_bucket(rel)[..., None] == jnp.arange(REL_BUCKETS)).astype(F32)
    bias = jnp.einsum("jib,bh->hji", onehot, rel_bias.astype(F32), precision=lax.Precision.HIGHEST)
    bias = jnp.where((jnp.abs(rel) <= WINDOW)[None], bias, -jnp.inf)
    return jnp.concatenate([bias, jnp.full((ATTN_HEADS, QBLK, QBLK), -jnp.inf, F32)], axis=1)


def _head_lanes(p):
    q = p.astype(F32).reshape(2, SSM_GROUPS, GROUP_HEADS).transpose(1, 0, 2).reshape(SSM_GROUPS, 2 * GROUP_HEADS)
    return jnp.pad(q, ((0, 0), (0, LANES - 2 * GROUP_HEADS)))[:, None, :]


def _expand_matrix():
    j = jnp.arange(LANES)[:, None]
    col = jnp.arange(2 * GROUP_W)[None, :]
    return jnp.where((j < 2 * GROUP_HEADS) & (col // HEAD_DIM == j), 1.0, 0.0).astype(BF16)


def kernel(x, rel_bias, norm1_w, w_in, conv_w, conv_b, dt_bias, a_log, d_skip, ssm_norm_w, attn_sink,
           w_out, norm2_w, w_up, ffn_conv_w, ffn_conv_b, w_down, final_norm_w):
    batch, seq, _ = x.shape
    assert seq % CHUNK == 0 and seq % QBLK == 0
    n = batch * seq
    depth = w_in.shape[0]
    x2 = x.reshape(n, D_MODEL)
    bias_t = _band_bias_t(rel_bias)
    e_mat = _expand_matrix()
    zw, xw_end = 1024, 2560
    dt_end = xw_end + 2 * SSM_HEADS
    q_end = dt_end + ATTN_HEADS * HEAD_DIM
    scale = HEAD_DIM ** -0.5
    nck = D_FF // FF_CHUNK

    for i in range(depth):
        wi = w_in[i]
        w_main = jnp.concatenate(
            [wi[:, :zw], wi[:, dt_end:q_end] * scale, wi[:, zw:xw_end], wi[:, q_end:]], axis=1).astype(BF16)
        wdt = wi[:, xw_end:dt_end].reshape(D_MODEL, 2, SSM_GROUPS, GROUP_HEADS).transpose(2, 0, 1, 3)
        wdt = jnp.pad(wdt.reshape(SSM_GROUPS, D_MODEL, 2 * GROUP_HEADS),
                      ((0, 0), (0, 0), (0, LANES - 2 * GROUP_HEADS))).astype(BF16)
        proj, dt = _inproj(x2, norm1_w[i][None], w_main, wdt)
        y_ssm = _ssd(proj, dt, conv_w[i], conv_b[i][None], _head_lanes(dt_bias[i]), _head_lanes(a_log[i]),
                     jnp.repeat(d_skip[i].astype(F32), HEAD_DIM)[None], e_mat, batch, seq)
        y_attn = _attn(proj, bias_t, attn_sink[i].astype(F32), batch, seq)
        wo = w_out[i].astype(BF16)
        x2 = _outproj(x2, y_ssm, proj, y_attn, ssm_norm_w[i][None], wo[:D_MODEL], wo[D_MODEL:])
        wup = w_up[i].astype(BF16).reshape(D_MODEL, 2 * nck, FF_CHUNK).transpose(1, 0, 2)
        x2 = _ffn(x2, norm2_w[i][None], wup, ffn_conv_w[i], ffn_conv_b[i][None],
                  w_down[i].astype(BF16), final_norm_w[None], batch, seq, final=(i == depth - 1))

    return x2.reshape(batch, seq, D_MODEL)
```
